```python
import math
import jax, jax.numpy as jnp
from jax import lax
import numpy as np

D_MODEL = 1024
BATCH = 8
SEQ = 8192
DEPTH = 1

CHUNK = 64
MIX_WIDTH = D_MODEL
RET_HEADS = 4
RET_HEAD_DIM = MIX_WIDTH // 8
RET_WIDTH = RET_HEADS * RET_HEAD_DIM
GMLP_GROUPS = 4
GMLP_GROUP_DIM = MIX_WIDTH // 8
GMLP_WIDTH = GMLP_GROUPS * GMLP_GROUP_DIM
GMLP_CHUNK = 128
IN_WIDTH = 4 * RET_WIDTH + 2 * GMLP_WIDTH
D_FF = 2816
ROPE_BASE = 10000.0
LN_EPS = 1e-5
DEEPNORM_ALPHA = (2.0 * DEPTH) ** 0.25
DEEPNORM_BETA = (8.0 * DEPTH) ** -0.25

kernel_name = "hybrid_retention_gmlp_macaron_deepnorm"


def layer_norm(x, g, b):
    xf = x.astype(jnp.float32)
    mu = jnp.mean(xf, axis=-1, keepdims=True)
    var = jnp.mean(jnp.square(xf - mu), axis=-1, keepdims=True)
    return ((xf - mu) * lax.rsqrt(var + LN_EPS) * g.astype(jnp.float32) + b.astype(jnp.float32)).astype(x.dtype)


def swiglu_ffn(x, w_in, w_out):
    gate, up = jnp.split(x @ w_in, 2, axis=-1)
    return (jax.nn.silu(gate) * up) @ w_out


def rotary(x, positions):
    d = x.shape[-1]
    inv_freq = ROPE_BASE ** (-jnp.arange(0, d, 2, dtype=jnp.float32) / d)
    ang = positions.astype(jnp.float32)[:, None] * inv_freq[None, :]
    cos = jnp.cos(ang)[None, :, None, :]
    sin = jnp.sin(ang)[None, :, None, :]
    x1, x2 = jnp.split(x.astype(jnp.float32), 2, axis=-1)
    return jnp.concatenate([x1 * cos - x2 * sin, x1 * sin + x2 * cos], axis=-1)


def chunkwise_retention(q, k, v):
    B_, S_, H, dk = q.shape
    dv = v.shape[-1]
    n = S_ // CHUNK
    log_gamma = jnp.log1p(-(2.0 ** (-5.0 - jnp.arange(H, dtype=jnp.float32))))
    idx = jnp.arange(CHUNK, dtype=jnp.float32)
    rel = idx[:, None] - idx[None, :]
    decay_intra = jnp.where(rel[None] >= 0,
                            jnp.exp(jnp.maximum(rel, 0.0)[None] * log_gamma[:, None, None]), 0.0)
    xi = jnp.exp((idx + 1.0)[None, :] * log_gamma[:, None])
    zeta = jnp.exp((CHUNK - 1.0 - idx)[None, :] * log_gamma[:, None])
    gamma_chunk = jnp.exp(CHUNK * log_gamma)

    qc = q.reshape(B_, n, CHUNK, H, dk)
    kc = k.reshape(B_, n, CHUNK, H, dk)
    vc = v.reshape(B_, n, CHUNK, H, dv)

    scores = jnp.einsum('bnihd,bnjhd->bnhij', qc, kc) * decay_intra[None, None]
    intra = jnp.einsum('bnhij,bnjhe->bnihe', scores, vc)

    kv = jnp.einsum('bnjhd,bnjhe->nbhde', kc, vc * zeta.T[None, None, :, :, None])

    def step(state, inp):
        q_i, kv_i = inp
        cross = jnp.einsum('bihd,bhde->bihe', q_i, state)
        return gamma_chunk[None, :, None, None] * state + kv_i, cross

    state0 = jnp.zeros((B_, H, dk, dv), jnp.float32)
    _, cross = lax.scan(step, state0, (jnp.moveaxis(qc, 1, 0), kv))
    cross = jnp.moveaxis(cross, 0, 1) * xi.T[None, None, :, :, None]
    return (intra + cross).reshape(B_, S_, H, dv)


def spatial_gating(u, z, g_v, b_v, w_s, b_s):
    B_, S_, G, c = z.shape
    n = S_ // GMLP_CHUNK
    z = layer_norm(z, g_v, b_v)
    mask = jnp.tril(jnp.ones((GMLP_CHUNK, GMLP_CHUNK), dtype=bool))
    w = jnp.where(mask[None], w_s, 0.0).astype(z.dtype)
    zc = z.reshape(B_, n, GMLP_CHUNK, G, c)
    mixed = jnp.einsum('gts,bnsgc->bntgc', w, zc) + b_s.T.astype(z.dtype)[None, None, :, :, None]
    return u * mixed.reshape(B_, S_, G, c)


def hybrid_mixer(h, w_in, ret_gn_g, ret_gn_b, gmlp_ln_g, gmlp_ln_b, gmlp_w_s, gmlp_b_s, w_out):
    B_, S_, _ = h.shape
    proj = h @ w_in
    q, k, v, g, u, z = jnp.split(
        proj, [RET_WIDTH, 2 * RET_WIDTH, 3 * RET_WIDTH, 4 * RET_WIDTH, 4 * RET_WIDTH + GMLP_WIDTH], axis=-1)

    positions = jnp.arange(S_, dtype=jnp.int32)
    qh = rotary(q.reshape(B_, S_, RET_HEADS, RET_HEAD_DIM), positions)
    kh = rotary(k.reshape(B_, S_, RET_HEADS, RET_HEAD_DIM), positions) * (RET_HEAD_DIM ** -0.5)
    vh = v.reshape(B_, S_, RET_HEADS, RET_HEAD_DIM).astype(jnp.float32)
    ret = chunkwise_retention(qh, kh, vh)
    ret = layer_norm(ret, ret_gn_g, ret_gn_b).astype(h.dtype)
    ret = jax.nn.silu(g) * ret.reshape(B_, S_, RET_WIDTH)

    u = jax.nn.gelu(u, approximate=False).reshape(B_, S_, GMLP_GROUPS, GMLP_GROUP_DIM)
    z = jax.nn.gelu(z, approximate=False).reshape(B_, S_, GMLP_GROUPS, GMLP_GROUP_DIM)
    sg = spatial_gating(u, z, gmlp_ln_g, gmlp_ln_b, gmlp_w_s, gmlp_b_s).reshape(B_, S_, GMLP_WIDTH)

    return jnp.concatenate([ret, sg], axis=-1) @ w_out


def setup_inputs(seed: int = 0) -> dict:
    key = jax.random.key(seed)
    ks = jax.random.split(key, 20)
    f32 = jnp.float32
    nrm = lambda k, shape, s: jax.random.normal(k, shape, f32) * s
    x = nrm(ks[0], (BATCH, SEQ, D_MODEL), 1.0)
    ffn1_w_in = nrm(ks[1], (DEPTH, D_MODEL, 2 * D_FF), D_MODEL ** -0.5)
    ffn1_w_out = nrm(ks[2], (DEPTH, D_FF, D_MODEL), D_FF ** -0.5 * DEEPNORM_BETA)
    ln1_g = 1.0 + nrm(ks[3], (DEPTH, D_MODEL), 0.02)
    ln1_b = nrm(ks[4], (DEPTH, D_MODEL), 0.02)
    mix_w_in = nrm(ks[5], (DEPTH, D_MODEL, IN_WIDTH), D_MODEL ** -0.5)
    mix_w_in = mix_w_in.at[:, :, 2 * RET_WIDTH:3 * RET_WIDTH].multiply(DEEPNORM_BETA)
    ret_gn_g = 1.0 + nrm(ks[6], (DEPTH, RET_HEADS, RET_HEAD_DIM), 0.02)
    ret_gn_b = nrm(ks[7], (DEPTH, RET_HEADS, RET_HEAD_DIM), 0.02)
    gmlp_ln_g = 1.0 + nrm(ks[8], (DEPTH, GMLP_GROUPS, GMLP_GROUP_DIM), 0.02)
    gmlp_ln_b = nrm(ks[9], (DEPTH, GMLP_GROUPS, GMLP_GROUP_DIM), 0.02)
    gmlp_w_s = nrm(ks[10], (DEPTH, GMLP_GROUPS, GMLP_CHUNK, GMLP_CHUNK), GMLP_CHUNK ** -0.5)
    gmlp_b_s = 1.0 + nrm(ks[11], (DEPTH, GMLP_GROUPS, GMLP_CHUNK), 0.02)
    mix_w_out = nrm(ks[12], (DEPTH, MIX_WIDTH, D_MODEL), MIX_WIDTH ** -0.5 * DEEPNORM_BETA)
    ln2_g = 1.0 + nrm(ks[13], (DEPTH, D_MODEL), 0.02)
    ln2_b = nrm(ks[14], (DEPTH, D_MODEL), 0.02)
    ffn2_w_in = nrm(ks[15], (DEPTH, D_MODEL, 2 * D_FF), D_MODEL ** -0.5)
    ffn2_w_out = nrm(ks[16], (DEPTH, D_FF, D_MODEL), D_FF ** -0.5 * DEEPNORM_BETA)
    ln3_g = 1.0 + nrm(ks[17], (DEPTH, D_MODEL), 0.02)
    ln3_b = nrm(ks[18], (DEPTH, D_MODEL), 0.02)
    return {"x": x, "ffn1_w_in": ffn1_w_in, "ffn1_w_out": ffn1_w_out, "ln1_g": ln1_g, "ln1_b": ln1_b,
            "mix_w_in": mix_w_in, "ret_gn_g": ret_gn_g, "ret_gn_b": ret_gn_b,
            "gmlp_ln_g": gmlp_ln_g, "gmlp_ln_b": gmlp_ln_b, "gmlp_w_s": gmlp_w_s, "gmlp_b_s": gmlp_b_s,
            "mix_w_out": mix_w_out, "ln2_g": ln2_g, "ln2_b": ln2_b,
            "ffn2_w_in": ffn2_w_in, "ffn2_w_out": ffn2_w_out, "ln3_g": ln3_g, "ln3_b": ln3_b}


def reference(x, ffn1_w_in, ffn1_w_out, ln1_g, ln1_b, mix_w_in, ret_gn_g, ret_gn_b,
              gmlp_ln_g, gmlp_ln_b, gmlp_w_s, gmlp_b_s, mix_w_out, ln2_g, ln2_b,
              ffn2_w_in, ffn2_w_out, ln3_g, ln3_b):
    for l in range(DEPTH):
        x = layer_norm(DEEPNORM_ALPHA * x + 0.5 * swiglu_ffn(x, ffn1_w_in[l], ffn1_w_out[l]), ln1_g[l], ln1_b[l])
        mix = hybrid_mixer(x, mix_w_in[l], ret_gn_g[l], ret_gn_b[l], gmlp_ln_g[l], gmlp_ln_b[l],
                           gmlp_w_s[l], gmlp_b_s[l], mix_w_out[l])
        x = layer_norm(DEEPNORM_ALPHA * x + mix, ln2_g[l], ln2_b[l])
        x = layer_norm(DEEPNORM_ALPHA * x + 0.5 * swiglu_ffn(x, ffn2_w_in[l], ffn2_w_out[l]), ln3_g[l], ln3_b[l])
    return x
```

```python
import functools

import jax
import jax.numpy as jnp
from jax import lax
from jax.experimental import pallas as pl
from jax.experimental.pallas import tpu as pltpu

F32 = jnp.float32
BF16 = jnp.bfloat16

D_MODEL = 1024
DEPTH = 1
CHUNK = 64
RET_HEADS = 4
HEAD_DIM = 128
RET_WIDTH = RET_HEADS * HEAD_DIM
GMLP_GROUPS = 4
GMLP_WIDTH = GMLP_GROUPS * HEAD_DIM
GMLP_CHUNK = 128
IN_WIDTH = 4 * RET_WIDTH + 2 * GMLP_WIDTH
D_FF = 2816
ROPE_BASE = 10000.0
LN_EPS = 1e-5
DEEPNORM_ALPHA = (2.0 * DEPTH) ** 0.25

LANES = 128
MXU_DIM = 256
VMEM_LIMIT_BYTES = 56 * 1024 * 1024

FFN_TOKENS = 512
FFN_CHUNK = MXU_DIM
MIX_TOKENS = 512
RET_BLOCK = 256


def _layer_norm(r, g, b):
    mu = jnp.mean(r, axis=-1, keepdims=True)
    d = r - mu
    var = jnp.mean(d * d, axis=-1, keepdims=True)
    return d * lax.rsqrt(var + LN_EPS) * g + b


def _dot(a, b):
    return jnp.dot(a, b, preferred_element_type=F32)


def _ffn_kernel(x_ref, win_ref, wout_ref, g_ref, b_ref, o_ref, act_ref):
    x = x_ref[...]
    xb = x.astype(BF16)
    for j in range(D_FF // FFN_CHUNK):
        lo = j * FFN_CHUNK
        gate = _dot(xb, win_ref[:, lo:lo + FFN_CHUNK])
        up = _dot(xb, win_ref[:, D_FF + lo:D_FF + lo + FFN_CHUNK])
        act_ref[:, lo:lo + FFN_CHUNK] = (gate * jax.nn.sigmoid(gate) * up).astype(BF16)
    y = _dot(act_ref[...], wout_ref[...])
    r = DEEPNORM_ALPHA * x + 0.5 * y
    o_ref[...] = _layer_norm(r, g_ref[...], b_ref[...])


def _ffn_call(x, w_in, w_out, ln_g, ln_b):
    n = x.shape[0]
    const = lambda i: (0, 0)
    resident = functools.partial(pl.BlockSpec, index_map=const, pipeline_mode=pl.Buffered(1))
    return pl.pallas_call(
        _ffn_kernel,
        out_shape=jax.ShapeDtypeStruct((n, D_MODEL), F32),
        grid=(n // FFN_TOKENS,),
        in_specs=[
            pl.BlockSpec((FFN_TOKENS, D_MODEL), lambda i: (i, 0)),
            resident((D_MODEL, 2 * D_FF)),
            resident((D_FF, D_MODEL)),
            resident((1, D_MODEL)),
            resident((1, D_MODEL)),
        ],
        out_specs=pl.BlockSpec((FFN_TOKENS, D_MODEL), lambda i: (i, 0)),
        scratch_shapes=[pltpu.VMEM((FFN_TOKENS, D_FF), BF16)],
        compiler_params=pltpu.CompilerParams(
            dimension_semantics=("arbitrary",), vmem_limit_bytes=VMEM_LIMIT_BYTES),
        name="ffn_ln",
    )(x, w_in, w_out, ln_g, ln_b)


def _rope(t, cos, sin_signed):
    return t * cos + pltpu.roll(t, HEAD_DIM // 2, 1) * sin_signed


def _gelu(t):
    return 0.5 * t * (1.0 + lax.erf(t * (0.5 ** 0.5)))


def _mixer_kernel(x_ref, win_ref, wout_ref, ropeq_ref, ropek_ref, decay_ref, xi_ref, zeta_ref,
                  gn_g_ref, gn_b_ref, zln_g_ref, zln_b_ref, ws_ref, bs_ref, ln_g_ref, ln_b_ref,
                  o_ref, state_ref, mix_ref, *, gamma_block):
    @pl.when(pl.program_id(1) == 0)
    def _():
        state_ref[...] = jnp.zeros_like(state_ref)

    x = x_ref[...]
    xb = x.astype(BF16)

    def proj(k):
        return _dot(xb, win_ref[:, k * RET_WIDTH:(k + 1) * RET_WIDTH])

    q, k, v, g, u, z = (proj(i) for i in range(6))

    for h in range(RET_HEADS):
        cols = slice(h * HEAD_DIM, (h + 1) * HEAD_DIM)
        state = state_ref[h]
        for c in range(MIX_TOKENS // RET_BLOCK):
            rows = slice(c * RET_BLOCK, (c + 1) * RET_BLOCK)
            qc = _rope(q[rows, cols], ropeq_ref[rows, :HEAD_DIM], ropeq_ref[rows, HEAD_DIM:]).astype(BF16)
            kc = _rope(k[rows, cols], ropek_ref[rows, :HEAD_DIM], ropek_ref[rows, HEAD_DIM:]).astype(BF16)
            vc = v[rows, cols]
            scores = lax.dot_general(qc, kc, (((1,), (1,)), ((), ())), preferred_element_type=F32)
            scores = scores * decay_ref[h]
            intra = _dot(scores.astype(BF16), vc.astype(BF16))
            cross = _dot(qc, state.astype(BF16)) * xi_ref[h]
            kv = lax.dot_general(kc, (vc * zeta_ref[h]).astype(BF16), (((0,), (0,)), ((), ())),
                                 preferred_element_type=F32)
            state = gamma_block[h] * state + kv
            ret = _layer_norm(intra + cross, gn_g_ref[:, cols], gn_b_ref[:, cols])
            gate = g[rows, cols]
            mix_ref[rows, cols] = (gate * jax.nn.sigmoid(gate) * ret).astype(BF16)
        state_ref[h] = state

    n_sub = MIX_TOKENS // GMLP_CHUNK
    row_id = lax.broadcasted_iota(jnp.int32, (GMLP_CHUNK, GMLP_CHUNK), 0)
    col_id = lax.broadcasted_iota(jnp.int32, (GMLP_CHUNK, GMLP_CHUNK), 1)
    causal = row_id >= col_id
    for gi in range(GMLP_GROUPS):
        cols = slice(gi * HEAD_DIM, (gi + 1) * HEAD_DIM)
        zn = _layer_norm(_gelu(z[:, cols]), zln_g_ref[:, cols], zln_b_ref[:, cols]).astype(BF16)
        zcat = jnp.concatenate([zn[s * GMLP_CHUNK:(s + 1) * GMLP_CHUNK] for s in range(n_sub)], axis=1)
        w = jnp.where(causal, ws_ref[gi], 0.0).astype(BF16)
        mixed = _dot(w, zcat)
        for s in range(n_sub):
            rows = slice(s * GMLP_CHUNK, (s + 1) * GMLP_CHUNK)
            m = mixed[:, s * GMLP_CHUNK:(s + 1) * GMLP_CHUNK] + bs_ref[gi]
            mix_ref[rows, RET_WIDTH + gi * HEAD_DIM:RET_WIDTH + (gi + 1) * HEAD_DIM] = (
                _gelu(u[rows, cols]) * m).astype(BF16)

    y = _dot(mix_ref[...], wout_ref[...])
    o_ref[...] = _layer_norm(DEEPNORM_ALPHA * x + y, ln_g_ref[...], ln_b_ref[...])


def _retention_tables(block):
    hh = jnp.arange(RET_HEADS, dtype=F32)
    log_gamma = jnp.log1p(-(2.0 ** (-5.0 - hh)))
    idx = jnp.arange(block, dtype=F32)
    rel = idx[:, None] - idx[None, :]
    decay = jnp.where(rel[None] >= 0, jnp.exp(jnp.maximum(rel, 0.0)[None] * log_gamma[:, None, None]), 0.0)
    xi = jnp.exp((idx + 1.0)[None, :] * log_gamma[:, None])
    zeta = jnp.exp((block - 1.0 - idx)[None, :] * log_gamma[:, None])
    widen = lambda t: jnp.broadcast_to(t[:, :, None], (RET_HEADS, block, HEAD_DIM))
    return decay, widen(xi), widen(zeta)


def _rope_tables(seq):
    inv_freq = ROPE_BASE ** (-jnp.arange(0, HEAD_DIM, 2, dtype=F32) / HEAD_DIM)
    ang = jnp.arange(seq, dtype=jnp.int32).astype(F32)[:, None] * inv_freq[None, :]
    cos, sin = jnp.cos(ang), jnp.sin(ang)
    table = jnp.concatenate([cos, cos, -sin, sin], axis=-1)
    return table, table * (HEAD_DIM ** -0.5)


def _mixer_call(x, batch, seq, w_in, w_out, gn_g, gn_b, zln_g, zln_b, w_s, b_s, ln_g, ln_b):
    n_seq = seq // MIX_TOKENS
    rope_q, rope_k = _rope_tables(seq)
    decay, xi, zeta = _retention_tables(RET_BLOCK)
    gamma_block = tuple(float((1.0 - 2.0 ** (-5.0 - h)) ** RET_BLOCK) for h in range(RET_HEADS))
    bs_full = jnp.broadcast_to(b_s[:, :, None], (GMLP_GROUPS, GMLP_CHUNK, HEAD_DIM))

    def resident(shape):
        zeros = (0,) * len(shape)
        return pl.BlockSpec(shape, lambda b, s: zeros, pipeline_mode=pl.Buffered(1))

    tok = pl.BlockSpec((MIX_TOKENS, D_MODEL), lambda b, s: (b * n_seq + s, 0))
    pos = pl.BlockSpec((MIX_TOKENS, 2 * HEAD_DIM), lambda b, s: (s, 0))
    return pl.pallas_call(
        functools.partial(_mixer_kernel, gamma_block=gamma_block),
        out_shape=jax.ShapeDtypeStruct(x.shape, F32),
        grid=(batch, n_seq),
        in_specs=[
            tok,
            resident((D_MODEL, IN_WIDTH)),
            resident((D_MODEL, D_MODEL)),
            pos, pos,
            resident((RET_HEADS, RET_BLOCK, RET_BLOCK)),
            resident((RET_HEADS, RET_BLOCK, HEAD_DIM)),
            resident((RET_HEADS, RET_BLOCK, HEAD_DIM)),
            resident((1, RET_WIDTH)), resident((1, RET_WIDTH)),
            resident((1, GMLP_WIDTH)), resident((1, GMLP_WIDTH)),
            resident((GMLP_GROUPS, GMLP_CHUNK, GMLP_CHUNK)),
            resident((GMLP_GROUPS, GMLP_CHUNK, HEAD_DIM)),
            resident((1, D_MODEL)), resident((1, D_MODEL)),
        ],
        out_specs=tok,
        scratch_shapes=[
            pltpu.VMEM((RET_HEADS, HEAD_DIM, HEAD_DIM), F32),
            pltpu.VMEM((MIX_TOKENS, D_MODEL), BF16),
        ],
        compiler_params=pltpu.CompilerParams(
            dimension_semantics=("arbitrary", "arbitrary"), vmem_limit_bytes=VMEM_LIMIT_BYTES),
        name="mixer_ln",
    )(x, w_in, w_out, rope_q, rope_k, decay, xi, zeta, gn_g, gn_b, zln_g, zln_b, w_s, bs_full, ln_g, ln_b)


def kernel(x, ffn1_w_in, ffn1_w_out, ln1_g, ln1_b, mix_w_in, ret_gn_g, ret_gn_b, gmlp_ln_g, gmlp_ln_b,
           gmlp_w_s, gmlp_b_s, mix_w_out, ln2_g, ln2_b, ffn2_w_in, ffn2_w_out, ln3_g, ln3_b):
    batch, seq, d = x.shape
    assert d == D_MODEL and seq % MIX_TOKENS == 0 and (batch * seq) % FFN_TOKENS == 0
    assert MIX_TOKENS % RET_BLOCK == 0 and RET_BLOCK % CHUNK == 0 and MIX_TOKENS % GMLP_CHUNK == 0
    row = lambda t: t.reshape(1, -1)
    h = x.reshape(batch * seq, d)
    for l in range(DEPTH):
        h = _ffn_call(h, ffn1_w_in[l].astype(BF16), ffn1_w_out[l].astype(BF16), row(ln1_g[l]), row(ln1_b[l]))
        h = _mixer_call(h, batch, seq, mix_w_in[l].astype(BF16), mix_w_out[l].astype(BF16),
                        row(ret_gn_g[l]), row(ret_gn_b[l]), row(gmlp_ln_g[l]), row(gmlp_ln_b[l]),
                        gmlp_w_s[l], gmlp_b_s[l], row(ln2_g[l]), row(ln2_b[l]))
        h = _ffn_call(h, ffn2_w_in[l].astype(BF16), ffn2_w_out[l].astype(BF16), row(ln3_g[l]), row(ln3_b[l]))
    return h.reshape(batch, seq, d)
```

```python
import functools

import jax
import jax.numpy as jnp
from jax import lax
from jax.experimental import pallas as pl
from jax.experimental.pallas import tpu as pltpu

F32 = jnp.float32
BF16 = jnp.bfloat16

D_MODEL = 1024
DEPTH = 1
CHUNK = 64
RET_HEADS = 4
HEAD_DIM = 128
RET_WIDTH = RET_HEADS * HEAD_DIM
GMLP_GROUPS = 4
GMLP_WIDTH = GMLP_GROUPS * HEAD_DIM
GMLP_CHUNK = 128
IN_WIDTH = 4 * RET_WIDTH + 2 * GMLP_WIDTH
D_FF = 2816
ROPE_BASE = 10000.0
LN_EPS = 1e-5
DEEPNORM_ALPHA = (2.0 * DEPTH) ** 0.25

LANES = 128
MXU_DIM = 256
VMEM_LIMIT_BYTES = 56 * 1024 * 1024

FFN_TOKENS = 512
FFN_CHUNK = MXU_DIM
MIX_TOKENS = 512
RET_BLOCK = 256


def _layer_norm(r, g, b):
    mu = jnp.mean(r, axis=-1, keepdims=True)
    d = r - mu
    var = jnp.mean(d * d, axis=-1, keepdims=True)
    return d * lax.rsqrt(var + LN_EPS) * g + b


def _dot(a, b):
    return jnp.dot(a, b, preferred_element_type=F32)


def _skewed_steps(start_current, finish_previous):
    i = pl.program_id(0)
    last = pl.num_programs(0) - 1

    @pl.when(i == 0)
    def _():
        start_current(True, lambda: None)

    @pl.when(jnp.logical_and(i > 0, i < last))
    def _():
        start_current(False, finish_previous)

    @pl.when(i == last)
    def _():
        finish_previous()


def _ffn_kernel(x_ref, win_ref, wout_ref, g_ref, b_ref, o_ref, act_ref, ax_ref):
    slot = pl.program_id(0) % 2

    def start_current(first_step, finish):
        del first_step
        finish()
        x = x_ref[...]
        xb = x.astype(BF16)
        for j in range(D_FF // FFN_CHUNK):
            lo = j * FFN_CHUNK
            gate = _dot(xb, win_ref[:, lo:lo + FFN_CHUNK])
            up = _dot(xb, win_ref[:, D_FF + lo:D_FF + lo + FFN_CHUNK])
            act_ref[slot, :, lo:lo + FFN_CHUNK] = (gate * jax.nn.sigmoid(gate) * up).astype(BF16)
        ax_ref[...] = DEEPNORM_ALPHA * x

    def finish_previous():
        y = _dot(act_ref[1 - slot], wout_ref[...])
        o_ref[...] = _layer_norm(ax_ref[...] + 0.5 * y, g_ref[...], b_ref[...])

    _skewed_steps(start_current, finish_previous)


def _ffn_call(x, w_in, w_out, ln_g, ln_b):
    n = x.shape[0]
    n_tiles = n // FFN_TOKENS
    const = lambda i: (0, 0)
    resident = functools.partial(pl.BlockSpec, index_map=const, pipeline_mode=pl.Buffered(1))
    return pl.pallas_call(
        _ffn_kernel,
        out_shape=jax.ShapeDtypeStruct((n, D_MODEL), F32),
        grid=(n_tiles + 1,),
        in_specs=[
            pl.BlockSpec((FFN_TOKENS, D_MODEL), lambda i: (jnp.minimum(i, n_tiles - 1), 0)),
            resident((D_MODEL, 2 * D_FF)),
            resident((D_FF, D_MODEL)),
            resident((1, D_MODEL)),
            resident((1, D_MODEL)),
        ],
        out_specs=pl.BlockSpec((FFN_TOKENS, D_MODEL), lambda i: (jnp.maximum(i - 1, 0), 0)),
        scratch_shapes=[pltpu.VMEM((2, FFN_TOKENS, D_FF), BF16), pltpu.VMEM((FFN_TOKENS, D_MODEL), F32)],
        compiler_params=pltpu.CompilerParams(
            dimension_semantics=("arbitrary",), vmem_limit_bytes=VMEM_LIMIT_BYTES),
        name="ffn_ln",
    )(x, w_in, w_out, ln_g, ln_b)


def _rope(t, cos, sin_signed):
    return t * cos + pltpu.roll(t, HEAD_DIM // 2, 1) * sin_signed


def _gelu(t):
    return 0.5 * t * (1.0 + lax.erf(t * (0.5 ** 0.5)))


def _mixer_kernel(x_ref, win_ref, wout_ref, ropeq_ref, ropek_ref, decay_ref, xi_ref, zeta_ref,
                  gn_g_ref, gn_b_ref, zln_g_ref, zln_b_ref, ws_ref, bs_ref, ln_g_ref, ln_b_ref,
                  o_ref, state_ref, mix_ref, ax_ref, *, gamma_block, n_seq):
    i = pl.program_id(0)
    slot = i % 2

    def start_current(first_step, finish):
        _mix_tile(x_ref, win_ref, ropeq_ref, ropek_ref, decay_ref, xi_ref, zeta_ref,
                  gn_g_ref, gn_b_ref, zln_g_ref, zln_b_ref, ws_ref, bs_ref, state_ref,
                  mix_ref.at[slot], ax_ref, finish,
                  first_of_sequence=True if first_step else i % n_seq == 0, gamma_block=gamma_block)

    def finish_previous():
        y = _dot(mix_ref[1 - slot], wout_ref[...])
        o_ref[...] = _layer_norm(ax_ref[...] + y, ln_g_ref[...], ln_b_ref[...])

    _skewed_steps(start_current, finish_previous)


def _mix_tile(x_ref, win_ref, ropeq_ref, ropek_ref, decay_ref, xi_ref, zeta_ref,
              gn_g_ref, gn_b_ref, zln_g_ref, zln_b_ref, ws_ref, bs_ref, state_ref, mix_ref, ax_ref, finish,
              *, first_of_sequence, gamma_block):
    x = x_ref[...]
    xb = x.astype(BF16)

    def proj(k):
        return _dot(xb, win_ref[:, k * RET_WIDTH:(k + 1) * RET_WIDTH])

    def retention_head(h, q, k, v, g):
        cols = slice(h * HEAD_DIM, (h + 1) * HEAD_DIM)
        if first_of_sequence is True:
            state = jnp.zeros((HEAD_DIM, HEAD_DIM), F32)
        else:
            state = state_ref[h]
            state = jnp.where(first_of_sequence, jnp.zeros_like(state), state)
        for c in range(MIX_TOKENS // RET_BLOCK):
            rows = slice(c * RET_BLOCK, (c + 1) * RET_BLOCK)
            qc = _rope(q[rows, cols], ropeq_ref[rows, :HEAD_DIM], ropeq_ref[rows, HEAD_DIM:]).astype(BF16)
            kc = _rope(k[rows, cols], ropek_ref[rows, :HEAD_DIM], ropek_ref[rows, HEAD_DIM:]).astype(BF16)
            vc = v[rows, cols]
            scores = lax.dot_general(qc, kc, (((1,), (1,)), ((), ())), preferred_element_type=F32)
            scores = scores * decay_ref[h]
            intra = _dot(scores.astype(BF16), vc.astype(BF16))
            cross = _dot(qc, state.astype(BF16)) * xi_ref[h]
            kv = lax.dot_general(kc, (vc * zeta_ref[h]).astype(BF16), (((0,), (0,)), ((), ())),
                                 preferred_element_type=F32)
            state = gamma_block[h] * state + kv
            ret = _layer_norm(intra + cross, gn_g_ref[:, cols], gn_b_ref[:, cols])
            gate = g[rows, cols]
            mix_ref[rows, cols] = (gate * jax.nn.sigmoid(gate) * ret).astype(BF16)
        state_ref[h] = state

    n_sub = MIX_TOKENS // GMLP_CHUNK
    row_id = lax.broadcasted_iota(jnp.int32, (GMLP_CHUNK, GMLP_CHUNK), 0)
    col_id = lax.broadcasted_iota(jnp.int32, (GMLP_CHUNK, GMLP_CHUNK), 1)
    causal = row_id >= col_id

    def gating_group(gi, u, z):
        cols = slice(gi * HEAD_DIM, (gi + 1) * HEAD_DIM)
        zn = _layer_norm(_gelu(z[:, cols]), zln_g_ref[:, cols], zln_b_ref[:, cols]).astype(BF16)
        zcat = jnp.concatenate([zn[s * GMLP_CHUNK:(s + 1) * GMLP_CHUNK] for s in range(n_sub)], axis=1)
        w = jnp.where(causal, ws_ref[gi], 0.0).astype(BF16)
        mixed = _dot(w, zcat)
        for s in range(n_sub):
            rows = slice(s * GMLP_CHUNK, (s + 1) * GMLP_CHUNK)
            m = mixed[:, s * GMLP_CHUNK:(s + 1) * GMLP_CHUNK] + bs_ref[gi]
            mix_ref[rows, RET_WIDTH + gi * HEAD_DIM:RET_WIDTH + (gi + 1) * HEAD_DIM] = (
                _gelu(u[rows, cols]) * m).astype(BF16)

    q, k, v = proj(0), proj(1), proj(2)
    finish()
    g = proj(3)
    retention_head(0, q, k, v, g)
    z = proj(5)
    retention_head(1, q, k, v, g)
    u = proj(4)
    retention_head(2, q, k, v, g)
    gating_group(0, u, z)
    gating_group(1, u, z)
    retention_head(3, q, k, v, g)
    gating_group(2, u, z)
    gating_group(3, u, z)
    ax_ref[...] = DEEPNORM_ALPHA * x


def _retention_tables(block):
    hh = jnp.arange(RET_HEADS, dtype=F32)
    log_gamma = jnp.log1p(-(2.0 ** (-5.0 - hh)))
    idx = jnp.arange(block, dtype=F32)
    rel = idx[:, None] - idx[None, :]
    decay = jnp.where(rel[None] >= 0, jnp.exp(jnp.maximum(rel, 0.0)[None] * log_gamma[:, None, None]), 0.0)
    xi = jnp.exp((idx + 1.0)[None, :] * log_gamma[:, None])
    zeta = jnp.exp((block - 1.0 - idx)[None, :] * log_gamma[:, None])
    widen = lambda t: jnp.broadcast_to(t[:, :, None], (RET_HEADS, block, HEAD_DIM))
    return decay, widen(xi), widen(zeta)


def _rope_tables(seq):
    inv_freq = ROPE_BASE ** (-jnp.arange(0, HEAD_DIM, 2, dtype=F32) / HEAD_DIM)
    ang = jnp.arange(seq, dtype=jnp.int32).astype(F32)[:, None] * inv_freq[None, :]
    cos, sin = jnp.cos(ang), jnp.sin(ang)
    table = jnp.concatenate([cos, cos, -sin, sin], axis=-1)
    return table, table * (HEAD_DIM ** -0.5)


def _mixer_call(x, batch, seq, w_in, w_out, gn_g, gn_b, zln_g, zln_b, w_s, b_s, ln_g, ln_b):
    n_seq = seq // MIX_TOKENS
    n_tiles = batch * n_seq
    rope_q, rope_k = _rope_tables(seq)
    decay, xi, zeta = _retention_tables(RET_BLOCK)
    gamma_block = tuple(float((1.0 - 2.0 ** (-5.0 - h)) ** RET_BLOCK) for h in range(RET_HEADS))
    bs_full = jnp.broadcast_to(b_s[:, :, None], (GMLP_GROUPS, GMLP_CHUNK, HEAD_DIM))

    def resident(shape):
        zeros = (0,) * len(shape)
        return pl.BlockSpec(shape, lambda i: zeros, pipeline_mode=pl.Buffered(1))

    tok_in = pl.BlockSpec((MIX_TOKENS, D_MODEL), lambda i: (jnp.minimum(i, n_tiles - 1), 0))
    tok_out = pl.BlockSpec((MIX_TOKENS, D_MODEL), lambda i: (jnp.maximum(i - 1, 0), 0))
    pos = pl.BlockSpec((MIX_TOKENS, 2 * HEAD_DIM), lambda i: (jnp.minimum(i, n_tiles - 1) % n_seq, 0))
    return pl.pallas_call(
        functools.partial(_mixer_kernel, gamma_block=gamma_block, n_seq=n_seq),
        out_shape=jax.ShapeDtypeStruct(x.shape, F32),
        grid=(n_tiles + 1,),
        in_specs=[
            tok_in,
            resident((D_MODEL, IN_WIDTH)),
            resident((D_MODEL, D_MODEL)),
            pos, pos,
            resident((RET_HEADS, RET_BLOCK, RET_BLOCK)),
            resident((RET_HEADS, RET_BLOCK, HEAD_DIM)),
            resident((RET_HEADS, RET_BLOCK, HEAD_DIM)),
            resident((1, RET_WIDTH)), resident((1, RET_WIDTH)),
            resident((1, GMLP_WIDTH)), resident((1, GMLP_WIDTH)),
            resident((GMLP_GROUPS, GMLP_CHUNK, GMLP_CHUNK)),
            resident((GMLP_GROUPS, GMLP_CHUNK, HEAD_DIM)),
            resident((1, D_MODEL)), resident((1, D_MODEL)),
        ],
        out_specs=tok_out,
        scratch_shapes=[
            pltpu.VMEM((RET_HEADS, HEAD_DIM, HEAD_DIM), F32),
            pltpu.VMEM((2, MIX_TOKENS, D_MODEL), BF16),
            pltpu.VMEM((MIX_TOKENS, D_MODEL), F32),
        ],
        compiler_params=pltpu.CompilerParams(
            dimension_semantics=("arbitrary",), vmem_limit_bytes=VMEM_LIMIT_BYTES),
        name="mixer_ln",
    )(x, w_in, w_out, rope_q, rope_k, decay, xi, zeta, gn_g, gn_b, zln_g, zln_b, w_s, bs_full, ln_g, ln_b)


def kernel(x, ffn1_w_in, ffn1_w_out, ln1_g, ln1_b, mix_w_in, ret_gn_g, ret_gn_b, gmlp_ln_g, gmlp_ln_b,
           gmlp_w_s, gmlp_b_s, mix_w_out, ln2_g, ln2_b, ffn2_w_in, ffn2_w_out, ln3_g, ln3_b):
    batch, seq, d = x.shape
    assert d == D_MODEL and seq % MIX_TOKENS == 0 and (batch * seq) % FFN_TOKENS == 0
    assert MIX_TOKENS % RET_BLOCK == 0 and RET_BLOCK % CHUNK == 0 and MIX_TOKENS % GMLP_CHUNK == 0
    row = lambda t: t.reshape(1, -1)
    h = x.reshape(batch * seq, d)
    for l in range(DEPTH):
        h = _ffn_call(h, ffn1_w_in[l].astype(BF16), ffn1_w_out[l].astype(BF16), row(ln1_g[l]), row(ln1_b[l]))
        h = _mixer_call(h, batch, seq, mix_w_in[l].astype(BF16), mix_w_out[l].astype(BF16),
                        row(ret_gn_g[l]), row(ret_gn_b[l]), row(gmlp_ln_g[l]), row(gmlp_ln_b[l]),
                        gmlp_w_s[l], gmlp_b_s[l], row(ln2_g[l]), row(ln2_b[l]))
        h = _ffn_call(h, ffn2_w_in[l].astype(BF16), ffn2_w_out[l].astype(BF16), row(ln3_g[l]), row(ln3_b[l]))
    return h.reshape(batch, seq, d)
```

```python
import functools

import jax
import jax.numpy as jnp
import numpy as np
from jax import lax
from jax.experimental import pallas as pl
from jax.experimental.pallas import tpu as pltpu

F32 = jnp.float32
BF16 = jnp.bfloat16

D_MODEL = 1024
DEPTH = 1
CHUNK = 64
RET_HEADS = 4
HEAD_DIM = 128
RET_WIDTH = RET_HEADS * HEAD_DIM
GMLP_GROUPS = 4
GMLP_WIDTH = GMLP_GROUPS * HEAD_DIM
GMLP_CHUNK = 128
IN_WIDTH = 4 * RET_WIDTH + 2 * GMLP_WIDTH
D_FF = 2816
ROPE_BASE = 10000.0
LN_EPS = 1e-5
DEEPNORM_ALPHA = (2.0 * DEPTH) ** 0.25

LANES = 128
MXU_DIM = 256
VMEM_LIMIT_BYTES = 56 * 1024 * 1024

FFN_TOKENS = 512
FFN_CHUNK = MXU_DIM
MIX_TOKENS = 512
RET_BLOCK = 256


def _layer_norm(r, g, b):
    mu = jnp.mean(r, axis=-1, keepdims=True)
    d = r - mu
    var = jnp.mean(d * d, axis=-1, keepdims=True)
    return d * lax.rsqrt(var + LN_EPS) * g + b


def _dot(a, b):
    return jnp.dot(a, b, preferred_element_type=F32)


def _skewed_steps(start_current, finish_previous):
    i = pl.program_id(0)
    last = pl.num_programs(0) - 1

    @pl.when(i == 0)
    def _():
        start_current(True, lambda: None)

    @pl.when(jnp.logical_and(i > 0, i < last))
    def _():
        start_current(False, finish_previous)

    @pl.when(i == last)
    def _():
        finish_previous()


def _ffn_kernel(x_ref, win_ref, wout_ref, g_ref, b_ref, o_ref, act_ref, ax_ref):
    def start_current(first_step, finish):
        del first_step
        finish()
        x = x_ref[...]
        xb = x.astype(BF16)
        for j in range(D_FF // FFN_CHUNK):
            lo = j * FFN_CHUNK
            gate = _dot(xb, win_ref[:, lo:lo + FFN_CHUNK])
            up = _dot(xb, win_ref[:, D_FF + lo:D_FF + lo + FFN_CHUNK])
            act_ref[:, lo:lo + FFN_CHUNK] = (gate * jax.nn.sigmoid(gate) * up).astype(BF16)
        ax_ref[...] = DEEPNORM_ALPHA * x

    def finish_previous():
        y = _dot(act_ref[...], wout_ref[...])
        o_ref[...] = _layer_norm(ax_ref[...] + 0.5 * y, g_ref[...], b_ref[...])

    _skewed_steps(start_current, finish_previous)


def _ffn_call(x, w_in, w_out, ln_g, ln_b):
    n = x.shape[0]
    n_tiles = n // FFN_TOKENS
    const = lambda i: (0, 0)
    resident = functools.partial(pl.BlockSpec, index_map=const, pipeline_mode=pl.Buffered(1))
    return pl.pallas_call(
        _ffn_kernel,
        out_shape=jax.ShapeDtypeStruct((n, D_MODEL), F32),
        grid=(n_tiles + 1,),
        in_specs=[
            pl.BlockSpec((FFN_TOKENS, D_MODEL), lambda i: (jnp.minimum(i, n_tiles - 1), 0)),
            resident((D_MODEL, 2 * D_FF)),
            resident((D_FF, D_MODEL)),
            resident((1, D_MODEL)),
            resident((1, D_MODEL)),
        ],
        out_specs=pl.BlockSpec((FFN_TOKENS, D_MODEL), lambda i: (jnp.maximum(i - 1, 0), 0)),
        scratch_shapes=[pltpu.VMEM((FFN_TOKENS, D_FF), BF16), pltpu.VMEM((FFN_TOKENS, D_MODEL), F32)],
        compiler_params=pltpu.CompilerParams(
            dimension_semantics=("arbitrary",), vmem_limit_bytes=VMEM_LIMIT_BYTES),
        name="ffn_ln",
    )(x, w_in, w_out, ln_g, ln_b)


def _rope(t, cos, sin_signed):
    return t * cos + pltpu.roll(t, HEAD_DIM // 2, 1) * sin_signed


def _gelu(t):
    return 0.5 * t * (1.0 + lax.erf(t * (0.5 ** 0.5)))


def _mixer_kernel(x_ref, win_ref, wout_ref, ropeq_ref, ropek_ref, decay_ref, xi_ref, zeta_ref,
                  gn_g_ref, gn_b_ref, zln_g_ref, zln_b_ref, ws_ref, bs_ref, ln_g_ref, ln_b_ref,
                  o_ref, state_ref, mix_ref, ax_ref, *, gamma_block, n_seq):
    i = pl.program_id(0)

    def start_current(first_step, finish):
        _mix_tile(x_ref, win_ref, ropeq_ref, ropek_ref, decay_ref, xi_ref, zeta_ref,
                  gn_g_ref, gn_b_ref, zln_g_ref, zln_b_ref, ws_ref, bs_ref, state_ref,
                  mix_ref, ax_ref, finish,
                  first_of_sequence=True if first_step else i % n_seq == 0, gamma_block=gamma_block)

    def finish_previous():
        y = _dot(mix_ref[...], wout_ref[...])
        o_ref[...] = _layer_norm(ax_ref[...] + y, ln_g_ref[...], ln_b_ref[...])

    _skewed_steps(start_current, finish_previous)


def _mix_tile(x_ref, win_ref, ropeq_ref, ropek_ref, decay_ref, xi_ref, zeta_ref,
              gn_g_ref, gn_b_ref, zln_g_ref, zln_b_ref, ws_ref, bs_ref, state_ref, mix_ref, ax_ref, finish,
              *, first_of_sequence, gamma_block):
    x = x_ref[...]
    xb = x.astype(BF16)

    def proj(k):
        return _dot(xb, win_ref[:, k * RET_WIDTH:(k + 1) * RET_WIDTH])

    def retention_head(h, q, k, v, g):
        cols = slice(h * HEAD_DIM, (h + 1) * HEAD_DIM)
        if first_of_sequence is True:
            state = jnp.zeros((HEAD_DIM, HEAD_DIM), F32)
        else:
            state = state_ref[h]
            state = jnp.where(first_of_sequence, jnp.zeros_like(state), state)
        for c in range(MIX_TOKENS // RET_BLOCK):
            rows = slice(c * RET_BLOCK, (c + 1) * RET_BLOCK)
            qc = _rope(q[rows, cols], ropeq_ref[rows, :HEAD_DIM], ropeq_ref[rows, HEAD_DIM:]).astype(BF16)
            kc = _rope(k[rows, cols], ropek_ref[rows, :HEAD_DIM], ropek_ref[rows, HEAD_DIM:]).astype(BF16)
            vc = v[rows, cols]
            scores = lax.dot_general(qc, kc, (((1,), (1,)), ((), ())), preferred_element_type=F32)
            scores = scores * decay_ref[h]
            intra = _dot(scores.astype(BF16), vc.astype(BF16))
            cross = _dot(qc, state.astype(BF16)) * xi_ref[h]
            kv = lax.dot_general(kc, (vc * zeta_ref[h]).astype(BF16), (((0,), (0,)), ((), ())),
                                 preferred_element_type=F32)
            state = gamma_block[h] * state + kv
            ret = _layer_norm(intra + cross, gn_g_ref[:, cols], gn_b_ref[:, cols])
            gate = g[rows, cols]
            mix_ref[rows, cols] = (gate * jax.nn.sigmoid(gate) * ret).astype(BF16)
        state_ref[h] = state

    n_sub = MIX_TOKENS // GMLP_CHUNK
    row_id = lax.broadcasted_iota(jnp.int32, (GMLP_CHUNK, GMLP_CHUNK), 0)
    col_id = lax.broadcasted_iota(jnp.int32, (GMLP_CHUNK, GMLP_CHUNK), 1)
    causal = row_id >= col_id

    def gating_group(gi, u, z):
        cols = slice(gi * HEAD_DIM, (gi + 1) * HEAD_DIM)
        zn = _layer_norm(_gelu(z[:, cols]), zln_g_ref[:, cols], zln_b_ref[:, cols]).astype(BF16)
        zcat = jnp.concatenate([zn[s * GMLP_CHUNK:(s + 1) * GMLP_CHUNK] for s in range(n_sub)], axis=1)
        w = jnp.where(causal, ws_ref[gi], 0.0).astype(BF16)
        mixed = _dot(w, zcat)
        for s in range(n_sub):
            rows = slice(s * GMLP_CHUNK, (s + 1) * GMLP_CHUNK)
            m = mixed[:, s * GMLP_CHUNK:(s + 1) * GMLP_CHUNK] + bs_ref[gi]
            mix_ref[rows, RET_WIDTH + gi * HEAD_DIM:RET_WIDTH + (gi + 1) * HEAD_DIM] = (
                _gelu(u[rows, cols]) * m).astype(BF16)

    q, k, v = proj(0), proj(1), proj(2)
    finish()
    g = proj(3)
    retention_head(0, q, k, v, g)
    z = proj(5)
    retention_head(1, q, k, v, g)
    u = proj(4)
    retention_head(2, q, k, v, g)
    gating_group(0, u, z)
    gating_group(1, u, z)
    retention_head(3, q, k, v, g)
    gating_group(2, u, z)
    gating_group(3, u, z)
    ax_ref[...] = DEEPNORM_ALPHA * x


def _retention_tables(block):
    log_gamma = np.log1p(-(2.0 ** (-5.0 - np.arange(RET_HEADS, dtype=np.float64))))
    idx = np.arange(block, dtype=np.float64)
    rel = idx[:, None] - idx[None, :]
    decay = np.where(rel[None] >= 0, np.exp(np.maximum(rel, 0.0)[None] * log_gamma[:, None, None]), 0.0)
    xi = np.exp((idx + 1.0)[None, :] * log_gamma[:, None])
    zeta = np.exp((block - 1.0 - idx)[None, :] * log_gamma[:, None])
    widen = lambda t: np.broadcast_to(t[:, :, None], (RET_HEADS, block, HEAD_DIM))
    return tuple(jnp.asarray(t, F32) for t in (decay, widen(xi), widen(zeta)))


def _rope_tables(seq):
    inv_freq = ROPE_BASE ** (-np.arange(0, HEAD_DIM, 2, dtype=np.float64) / HEAD_DIM)
    ang = np.arange(seq, dtype=np.float64)[:, None] * inv_freq[None, :]
    cos, sin = np.cos(ang), np.sin(ang)
    table = np.concatenate([cos, cos, -sin, sin], axis=-1)
    return jnp.asarray(table, F32), jnp.asarray(table * (HEAD_DIM ** -0.5), F32)


def _mixer_call(x, batch, seq, w_in, w_out, gn_g, gn_b, zln_g, zln_b, w_s, b_s, ln_g, ln_b):
    n_seq = seq // MIX_TOKENS
    n_tiles = batch * n_seq
    rope_q, rope_k = _rope_tables(seq)
    decay, xi, zeta = _retention_tables(RET_BLOCK)
    gamma_block = tuple(float((1.0 - 2.0 ** (-5.0 - h)) ** RET_BLOCK) for h in range(RET_HEADS))
    bs_full = jnp.broadcast_to(b_s[:, :, None], (GMLP_GROUPS, GMLP_CHUNK, HEAD_DIM))

    def resident(shape):
        zeros = (0,) * len(shape)
        return pl.BlockSpec(shape, lambda i: zeros, pipeline_mode=pl.Buffered(1))

    tok_in = pl.BlockSpec((MIX_TOKENS, D_MODEL), lambda i: (jnp.minimum(i, n_tiles - 1), 0))
    tok_out = pl.BlockSpec((MIX_TOKENS, D_MODEL), lambda i: (jnp.maximum(i - 1, 0), 0))
    pos = pl.BlockSpec((MIX_TOKENS, 2 * HEAD_DIM), lambda i: (jnp.minimum(i, n_tiles - 1) % n_seq, 0))
    return pl.pallas_call(
        functools.partial(_mixer_kernel, gamma_block=gamma_block, n_seq=n_seq),
        out_shape=jax.ShapeDtypeStruct(x.shape, F32),
        grid=(n_tiles + 1,),
        in_specs=[
            tok_in,
            resident((D_MODEL, IN_WIDTH)),
            resident((D_MODEL, D_MODEL)),
            pos, pos,
            resident((RET_HEADS, RET_BLOCK, RET_BLOCK)),
            resident((RET_HEADS, RET_BLOCK, HEAD_DIM)),
            resident((RET_HEADS, RET_BLOCK, HEAD_DIM)),
            resident((1, RET_WIDTH)), resident((1, RET_WIDTH)),
            resident((1, GMLP_WIDTH)), resident((1, GMLP_WIDTH)),
            resident((GMLP_GROUPS, GMLP_CHUNK, GMLP_CHUNK)),
            resident((GMLP_GROUPS, GMLP_CHUNK, HEAD_DIM)),
            resident((1, D_MODEL)), resident((1, D_MODEL)),
        ],
        out_specs=tok_out,
        scratch_shapes=[
            pltpu.VMEM((RET_HEADS, HEAD_DIM, HEAD_DIM), F32),
            pltpu.VMEM((MIX_TOKENS, D_MODEL), BF16),
            pltpu.VMEM((MIX_TOKENS, D_MODEL), F32),
        ],
        compiler_params=pltpu.CompilerParams(
            dimension_semantics=("arbitrary",), vmem_limit_bytes=VMEM_LIMIT_BYTES),
        name="mixer_ln",
    )(x, w_in, w_out, rope_q, rope_k, decay, xi, zeta, gn_g, gn_b, zln_g, zln_b, w_s, bs_full, ln_g, ln_b)


def kernel(x, ffn1_w_in, ffn1_w_out, ln1_g, ln1_b, mix_w_in, ret_gn_g, ret_gn_b, gmlp_ln_g, gmlp_ln_b,
           gmlp_w_s, gmlp_b_s, mix_w_out, ln2_g, ln2_b, ffn2_w_in, ffn2_w_out, ln3_g, ln3_b):
    batch, seq, d = x.shape
    assert d == D_MODEL and seq % MIX_TOKENS == 0 and (batch * seq) % FFN_TOKENS == 0
    assert MIX_TOKENS % RET_BLOCK == 0 and RET_BLOCK % CHUNK == 0 and MIX_TOKENS % GMLP_CHUNK == 0
    row = lambda t: t.reshape(1, -1)
    h = x.reshape(batch * seq, d)
    for l in range(DEPTH):
        h = _ffn_call(h, ffn1_w_in[l].astype(BF16), ffn1_w_out[l].astype(BF16), row(ln1_g[l]), row(ln1_b[l]))
        h = _mixer_call(h, batch, seq, mix_w_in[l].astype(BF16), mix_w_out[l].astype(BF16),
                        row(ret_gn_g[l]), row(ret_gn_b[l]), row(gmlp_ln_g[l]), row(gmlp_ln_b[l]),
                        gmlp_w_s[l], gmlp_b_s[l], row(ln2_g[l]), row(ln2_b[l]))
        h = _ffn_call(h, ffn2_w_in[l].astype(BF16), ffn2_w_out[l].astype(BF16), row(ln3_g[l]), row(ln3_b[l]))
    return h.reshape(batch, seq, d)
```

```python
import functools

import jax
import jax.numpy as jnp
import numpy as np
from jax import lax
from jax.experimental import pallas as pl
from jax.experimental.pallas import tpu as pltpu

F32 = jnp.float32
BF16 = jnp.bfloat16

D_MODEL = 1024
DEPTH = 1
CHUNK = 64
RET_HEADS = 4
HEAD_DIM = 128
RET_WIDTH = RET_HEADS * HEAD_DIM
GMLP_GROUPS = 4
GMLP_WIDTH = GMLP_GROUPS * HEAD_DIM
GMLP_CHUNK = 128
IN_WIDTH = 4 * RET_WIDTH + 2 * GMLP_WIDTH
D_FF = 2816
ROPE_BASE = 10000.0
LN_EPS = 1e-5
DEEPNORM_ALPHA = (2.0 * DEPTH) ** 0.25

LANES = 128
SUBLANES = 8
MXU_DIM = 256
VMEM_LIMIT_BYTES = 56 * 1024 * 1024

FFN_TOKENS = 512
FFN_CHUNK = MXU_DIM
MIX_TOKENS = 512
RET_BLOCK = 256
NORM_ROWS = SUBLANES
FFN_FREE_CHUNKS = 2


def _layer_norm(r, g, b):
    mu = jnp.mean(r, axis=-1, keepdims=True)
    d = r - mu
    var = jnp.mean(d * d, axis=-1, keepdims=True)
    return d * lax.rsqrt(var + LN_EPS) * g + b


def _dot(a, b):
    return jnp.dot(a, b, preferred_element_type=F32)


def _skewed_steps(start_current, finish_previous):
    i = pl.program_id(0)
    last = pl.num_programs(0) - 1

    @pl.when(i == 0)
    def _():
        start_current(True, lambda: [])

    @pl.when(jnp.logical_and(i > 0, i < last))
    def _():
        start_current(False, finish_previous)

    @pl.when(i == last)
    def _():
        for norm_rows in finish_previous():
            norm_rows(None)


def _zero_after(t):
    bits = lax.bitcast_convert_type(t[:SUBLANES, :LANES], jnp.int32)
    half = jnp.int32(16)
    return lax.shift_right_logical(lax.shift_right_logical(bits, half), half).astype(F32)


def _row_group_norms(resid_ref, y, scale, g_ref, b_ref, o_ref):
    def norm_rows(rows, anchor, next_resid=None):
        zeros = 0.0 if anchor is None else jnp.concatenate([anchor] * (y.shape[1] // LANES), axis=1)
        resid = resid_ref[rows, :] + zeros
        o_ref[rows, :] = _layer_norm(resid + scale * y[rows, :], g_ref[...], b_ref[...])
        if next_resid is not None:
            resid_ref[rows, :] = next_resid(rows) + zeros
    return [functools.partial(norm_rows, slice(lo, lo + NORM_ROWS)) for lo in range(0, y.shape[0], NORM_ROWS)]


def _deal(items, n_hands):
    return [items[h * len(items) // n_hands:(h + 1) * len(items) // n_hands] for h in range(n_hands)]


def _ffn_kernel(x_ref, win_ref, wout_ref, g_ref, b_ref, o_ref, act_ref, ax_ref):
    n_chunks = D_FF // FFN_CHUNK

    def scaled_x(rows):
        return DEEPNORM_ALPHA * x_ref[rows, :]

    def start_current(first_step, finish):
        norms = _deal(finish(), n_chunks - FFN_FREE_CHUNKS) + [[]] * FFN_FREE_CHUNKS
        xb = x_ref[...].astype(BF16)
        for j in range(n_chunks):
            lo = j * FFN_CHUNK
            gate = _dot(xb, win_ref[:, lo:lo + FFN_CHUNK])
            up = _dot(xb, win_ref[:, D_FF + lo:D_FF + lo + FFN_CHUNK])
            act_ref[:, lo:lo + FFN_CHUNK] = (gate * jax.nn.sigmoid(gate) * up).astype(BF16)
            for norm_rows in norms[j]:
                norm_rows(_zero_after(gate), scaled_x)
        if first_step:
            ax_ref[...] = scaled_x(slice(None))

    def finish_previous():
        y = _dot(act_ref[...], wout_ref[...])
        return _row_group_norms(ax_ref, y, 0.5, g_ref, b_ref, o_ref)

    _skewed_steps(start_current, finish_previous)


def _ffn_call(x, w_in, w_out, ln_g, ln_b):
    n = x.shape[0]
    n_tiles = n // FFN_TOKENS
    const = lambda i: (0, 0)
    resident = functools.partial(pl.BlockSpec, index_map=const, pipeline_mode=pl.Buffered(1))
    return pl.pallas_call(
        _ffn_kernel,
        out_shape=jax.ShapeDtypeStruct((n, D_MODEL), F32),
        grid=(n_tiles + 1,),
        in_specs=[
            pl.BlockSpec((FFN_TOKENS, D_MODEL), lambda i: (jnp.minimum(i, n_tiles - 1), 0)),
            resident((D_MODEL, 2 * D_FF)),
            resident((D_FF, D_MODEL)),
            resident((1, D_MODEL)),
            resident((1, D_MODEL)),
        ],
        out_specs=pl.BlockSpec((FFN_TOKENS, D_MODEL), lambda i: (jnp.maximum(i - 1, 0), 0)),
        scratch_shapes=[pltpu.VMEM((FFN_TOKENS, D_FF), BF16), pltpu.VMEM((FFN_TOKENS, D_MODEL), F32)],
        compiler_params=pltpu.CompilerParams(
            dimension_semantics=("arbitrary",), vmem_limit_bytes=VMEM_LIMIT_BYTES),
        name="ffn_ln",
    )(x, w_in, w_out, ln_g, ln_b)


def _rope(t, cos, sin_signed):
    return t * cos + pltpu.roll(t, HEAD_DIM // 2, 1) * sin_signed


def _gelu(t):
    return 0.5 * t * (1.0 + lax.erf(t * (0.5 ** 0.5)))


def _mixer_kernel(x_ref, win_ref, wout_ref, ropeq_ref, ropek_ref, decay_ref, xi_ref, zeta_ref,
                  gn_g_ref, gn_b_ref, zln_g_ref, zln_b_ref, ws_ref, bs_ref, ln_g_ref, ln_b_ref,
                  o_ref, state_ref, mix_ref, ax_ref, *, gamma_block, n_seq):
    i = pl.program_id(0)

    def start_current(first_step, finish):
        _mix_tile(x_ref, win_ref, ropeq_ref, ropek_ref, decay_ref, xi_ref, zeta_ref,
                  gn_g_ref, gn_b_ref, zln_g_ref, zln_b_ref, ws_ref, bs_ref, state_ref,
                  mix_ref, ax_ref, finish,
                  first_step=first_step, first_of_sequence=i % n_seq == 0, gamma_block=gamma_block)

    def finish_previous():
        y = _dot(mix_ref[...], wout_ref[...])
        return _row_group_norms(ax_ref, y, 1.0, ln_g_ref, ln_b_ref, o_ref)

    _skewed_steps(start_current, finish_previous)


def _mix_tile(x_ref, win_ref, ropeq_ref, ropek_ref, decay_ref, xi_ref, zeta_ref,
              gn_g_ref, gn_b_ref, zln_g_ref, zln_b_ref, ws_ref, bs_ref, state_ref, mix_ref, ax_ref, finish,
              *, first_step, first_of_sequence, gamma_block):
    xb = x_ref[...].astype(BF16)

    def proj(k):
        return _dot(xb, win_ref[:, k * RET_WIDTH:(k + 1) * RET_WIDTH])

    def scaled_x(rows):
        return DEEPNORM_ALPHA * x_ref[rows, :]

    def retention_head(h, q, k, v, g):
        cols = slice(h * HEAD_DIM, (h + 1) * HEAD_DIM)
        if first_step:
            state = jnp.zeros((HEAD_DIM, HEAD_DIM), F32)
        else:
            state = state_ref[h]
            state = jnp.where(first_of_sequence, jnp.zeros_like(state), state)
        for c in range(MIX_TOKENS // RET_BLOCK):
            rows = slice(c * RET_BLOCK, (c + 1) * RET_BLOCK)
            qc = _rope(q[rows, cols], ropeq_ref[rows, :HEAD_DIM], ropeq_ref[rows, HEAD_DIM:]).astype(BF16)
            kc = _rope(k[rows, cols], ropek_ref[rows, :HEAD_DIM], ropek_ref[rows, HEAD_DIM:]).astype(BF16)
            vc = v[rows, cols]
            scores = lax.dot_general(qc, kc, (((1,), (1,)), ((), ())), preferred_element_type=F32)
            scores = scores * decay_ref[h]
            intra = _dot(scores.astype(BF16), vc.astype(BF16))
            cross = _dot(qc, state.astype(BF16)) * xi_ref[h]
            kv = lax.dot_general(kc, (vc * zeta_ref[h]).astype(BF16), (((0,), (0,)), ((), ())),
                                 preferred_element_type=F32)
            state = gamma_block[h] * state + kv
            ret = _layer_norm(intra + cross, gn_g_ref[:, cols], gn_b_ref[:, cols])
            gate = g[rows, cols]
            mix_ref[rows, cols] = (gate * jax.nn.sigmoid(gate) * ret).astype(BF16)
        state_ref[h] = state

    n_sub = MIX_TOKENS // GMLP_CHUNK
    row_id = lax.broadcasted_iota(jnp.int32, (GMLP_CHUNK, GMLP_CHUNK), 0)
    col_id = lax.broadcasted_iota(jnp.int32, (GMLP_CHUNK, GMLP_CHUNK), 1)
    causal = row_id >= col_id

    def gating_group(gi, u, z):
        cols = slice(gi * HEAD_DIM, (gi + 1) * HEAD_DIM)
        zn = _layer_norm(_gelu(z[:, cols]), zln_g_ref[:, cols], zln_b_ref[:, cols]).astype(BF16)
        zcat = jnp.concatenate([zn[s * GMLP_CHUNK:(s + 1) * GMLP_CHUNK] for s in range(n_sub)], axis=1)
        w = jnp.where(causal, ws_ref[gi], 0.0).astype(BF16)
        mixed = _dot(w, zcat)
        for s in range(n_sub):
            rows = slice(s * GMLP_CHUNK, (s + 1) * GMLP_CHUNK)
            m = mixed[:, s * GMLP_CHUNK:(s + 1) * GMLP_CHUNK] + bs_ref[gi]
            mix_ref[rows, RET_WIDTH + gi * HEAD_DIM:RET_WIDTH + (gi + 1) * HEAD_DIM] = (
                _gelu(u[rows, cols]) * m).astype(BF16)

    q, k, v = proj(0), proj(1), proj(2)
    for norm_rows in finish():
        norm_rows(None, scaled_x)
    g = proj(3)
    retention_head(0, q, k, v, g)
    z = proj(5)
    retention_head(1, q, k, v, g)
    u = proj(4)
    retention_head(2, q, k, v, g)
    gating_group(0, u, z)
    gating_group(1, u, z)
    retention_head(3, q, k, v, g)
    gating_group(2, u, z)
    gating_group(3, u, z)
    if first_step:
        ax_ref[...] = scaled_x(slice(None))


def _retention_tables(block):
    log_gamma = np.log1p(-(2.0 ** (-5.0 - np.arange(RET_HEADS, dtype=np.float64))))
    idx = np.arange(block, dtype=np.float64)
    rel = idx[:, None] - idx[None, :]
    decay = np.where(rel[None] >= 0, np.exp(np.maximum(rel, 0.0)[None] * log_gamma[:, None, None]), 0.0)
    xi = np.exp((idx + 1.0)[None, :] * log_gamma[:, None])
    zeta = np.exp((block - 1.0 - idx)[None, :] * log_gamma[:, None])
    widen = lambda t: np.broadcast_to(t[:, :, None], (RET_HEADS, block, HEAD_DIM))
    return tuple(jnp.asarray(t, F32) for t in (decay, widen(xi), widen(zeta)))


def _rope_tables(seq):
    inv_freq = ROPE_BASE ** (-np.arange(0, HEAD_DIM, 2, dtype=np.float64) / HEAD_DIM)
    ang = np.arange(seq, dtype=np.float64)[:, None] * inv_freq[None, :]
    cos, sin = np.cos(ang), np.sin(ang)
    table = np.concatenate([cos, cos, -sin, sin], axis=-1)
    return jnp.asarray(table, F32), jnp.asarray(table * (HEAD_DIM ** -0.5), F32)


def _mixer_call(x, batch, seq, w_in, w_out, gn_g, gn_b, zln_g, zln_b, w_s, b_s, ln_g, ln_b):
    n_seq = seq // MIX_TOKENS
    n_tiles = batch * n_seq
    rope_q, rope_k = _rope_tables(seq)
    decay, xi, zeta = _retention_tables(RET_BLOCK)
    gamma_block = tuple(float((1.0 - 2.0 ** (-5.0 - h)) ** RET_BLOCK) for h in range(RET_HEADS))
    bs_full = jnp.broadcast_to(b_s[:, :, None], (GMLP_GROUPS, GMLP_CHUNK, HEAD_DIM))

    def resident(shape):
        zeros = (0,) * len(shape)
        return pl.BlockSpec(shape, lambda i: zeros, pipeline_mode=pl.Buffered(1))

    tok_in = pl.BlockSpec((MIX_TOKENS, D_MODEL), lambda i: (jnp.minimum(i, n_tiles - 1), 0))
    tok_out = pl.BlockSpec((MIX_TOKENS, D_MODEL), lambda i: (jnp.maximum(i - 1, 0), 0))
    pos = pl.BlockSpec((MIX_TOKENS, 2 * HEAD_DIM), lambda i: (jnp.minimum(i, n_tiles - 1) % n_seq, 0))
    return pl.pallas_call(
        functools.partial(_mixer_kernel, gamma_block=gamma_block, n_seq=n_seq),
        out_shape=jax.ShapeDtypeStruct(x.shape, F32),
        grid=(n_tiles + 1,),
        in_specs=[
            tok_in,
            resident((D_MODEL, IN_WIDTH)),
            resident((D_MODEL, D_MODEL)),
            pos, pos,
            resident((RET_HEADS, RET_BLOCK, RET_BLOCK)),
            resident((RET_HEADS, RET_BLOCK, HEAD_DIM)),
            resident((RET_HEADS, RET_BLOCK, HEAD_DIM)),
            resident((1, RET_WIDTH)), resident((1, RET_WIDTH)),
            resident((1, GMLP_WIDTH)), resident((1, GMLP_WIDTH)),
            resident((GMLP_GROUPS, GMLP_CHUNK, GMLP_CHUNK)),
            resident((GMLP_GROUPS, GMLP_CHUNK, HEAD_DIM)),
            resident((1, D_MODEL)), resident((1, D_MODEL)),
        ],
        out_specs=tok_out,
        scratch_shapes=[
            pltpu.VMEM((RET_HEADS, HEAD_DIM, HEAD_DIM), F32),
            pltpu.VMEM((MIX_TOKENS, D_MODEL), BF16),
            pltpu.VMEM((MIX_TOKENS, D_MODEL), F32),
        ],
        compiler_params=pltpu.CompilerParams(
            dimension_semantics=("arbitrary",), vmem_limit_bytes=VMEM_LIMIT_BYTES),
        name="mixer_ln",
    )(x, w_in, w_out, rope_q, rope_k, decay, xi, zeta, gn_g, gn_b, zln_g, zln_b, w_s, bs_full, ln_g, ln_b)


def kernel(x, ffn1_w_in, ffn1_w_out, ln1_g, ln1_b, mix_w_in, ret_gn_g, ret_gn_b, gmlp_ln_g, gmlp_ln_b,
           gmlp_w_s, gmlp_b_s, mix_w_out, ln2_g, ln2_b, ffn2_w_in, ffn2_w_out, ln3_g, ln3_b):
    batch, seq, d = x.shape
    assert d == D_MODEL and seq % MIX_TOKENS == 0 and (batch * seq) % FFN_TOKENS == 0
    assert MIX_TOKENS % RET_BLOCK == 0 and RET_BLOCK % CHUNK == 0 and MIX_TOKENS % GMLP_CHUNK == 0
    row = lambda t: t.reshape(1, -1)
    h = x.reshape(batch * seq, d)
    for l in range(DEPTH):
        h = _ffn_call(h, ffn1_w_in[l].astype(BF16), ffn1_w_out[l].astype(BF16), row(ln1_g[l]), row(ln1_b[l]))
        h = _mixer_call(h, batch, seq, mix_w_in[l].astype(BF16), mix_w_out[l].astype(BF16),
                        row(ret_gn_g[l]), row(ret_gn_b[l]), row(gmlp_ln_g[l]), row(gmlp_ln_b[l]),
                        gmlp_w_s[l], gmlp_b_s[l], row(ln2_g[l]), row(ln2_b[l]))
        h = _ffn_call(h, ffn2_w_in[l].astype(BF16), ffn2_w_out[l].astype(BF16), row(ln3_g[l]), row(ln3_b[l]))
    return h.reshape(batch, seq, d)
```

```python
import functools

import jax
import jax.numpy as jnp
import numpy as np
from jax import lax
from jax.experimental import pallas as pl
from jax.experimental.pallas import tpu as pltpu

F32 = jnp.float32
BF16 = jnp.bfloat16

D_MODEL = 1024
DEPTH = 1
CHUNK = 64
RET_HEADS = 4
HEAD_DIM = 128
RET_WIDTH = RET_HEADS * HEAD_DIM
GMLP_GROUPS = 4
GMLP_WIDTH = GMLP_GROUPS * HEAD_DIM
GMLP_CHUNK = 128
IN_WIDTH = 4 * RET_WIDTH + 2 * GMLP_WIDTH
D_FF = 2816
ROPE_BASE = 10000.0
LN_EPS = 1e-5
DEEPNORM_ALPHA = (2.0 * DEPTH) ** 0.25

LANES = 128
SUBLANES = 8
MXU_DIM = 256
VMEM_LIMIT_BYTES = 56 * 1024 * 1024

FFN_TOKENS = 512
FFN_CHUNK = MXU_DIM
MIX_TOKENS = 512
RET_BLOCK = 256
NORM_ROWS = SUBLANES
FFN_FREE_CHUNKS = 2
FFN_LOAD_STEPS = 11
MIX_LOAD_STEPS = 6
MIX_OUT_CHUNKS = 4


def _layer_norm(r, g, b):
    mu = jnp.mean(r, axis=-1, keepdims=True)
    d = r - mu
    var = jnp.mean(d * d, axis=-1, keepdims=True)
    return d * lax.rsqrt(var + LN_EPS) * g + b


def _dot(a, b):
    return jnp.dot(a, b, preferred_element_type=F32)


def _skewed_steps(n_load, load_weights, start_current, finish_previous):
    i = pl.program_id(0)
    t = i - n_load
    last = pl.num_programs(0) - 1 - n_load
    load_weights(i)

    @pl.when(t == 0)
    def _():
        start_current(True, lambda: [])

    @pl.when(jnp.logical_and(t > 0, t < last))
    def _():
        start_current(False, finish_previous)

    @pl.when(t == last)
    def _():
        for norm_rows in finish_previous():
            norm_rows(None)


def _cast_chunk(step, src_ref, dst_ref, axis, n_chunks):
    size = src_ref.shape[axis]
    for c in range(n_chunks):
        @pl.when(step == c)
        def _(c=c):
            where = (slice(None),) * axis + (slice(c * size, (c + 1) * size),)
            dst_ref[where] = src_ref[...].astype(BF16)


def _zero_after(t):
    bits = lax.bitcast_convert_type(t[:SUBLANES, :LANES], jnp.int32)
    half = jnp.int32(16)
    return lax.shift_right_logical(lax.shift_right_logical(bits, half), half).astype(F32)


def _row_group_norms(resid_ref, y, scale, g_ref, b_ref, o_ref):
    def norm_rows(rows, anchor, next_resid=None):
        zeros = 0.0 if anchor is None else jnp.concatenate([anchor] * (y.shape[1] // LANES), axis=1)
        resid = resid_ref[rows, :] + zeros
        o_ref[rows, :] = _layer_norm(resid + scale * y[rows, :], g_ref[...], b_ref[...])
        if next_resid is not None:
            resid_ref[rows, :] = next_resid(rows) + zeros
    return [functools.partial(norm_rows, slice(lo, lo + NORM_ROWS)) for lo in range(0, y.shape[0], NORM_ROWS)]


def _deal(items, n_hands):
    return [items[h * len(items) // n_hands:(h + 1) * len(items) // n_hands] for h in range(n_hands)]


def _ffn_kernel(x_ref, win32_ref, wout32_ref, g_ref, b_ref, o_ref, win_ref, wout_ref, act_ref, ax_ref):
    n_chunks = D_FF // FFN_CHUNK

    def load_weights(step):
        _cast_chunk(step, win32_ref, win_ref, 1, FFN_LOAD_STEPS)
        _cast_chunk(step, wout32_ref, wout_ref, 0, FFN_LOAD_STEPS)

    def scaled_x(rows):
        return DEEPNORM_ALPHA * x_ref[rows, :]

    def start_current(first_step, finish):
        norms = _deal(finish(), n_chunks - FFN_FREE_CHUNKS) + [[]] * FFN_FREE_CHUNKS
        xb = x_ref[...].astype(BF16)
        for j in range(n_chunks):
            lo = j * FFN_CHUNK
            gate = _dot(xb, win_ref[:, lo:lo + FFN_CHUNK])
            up = _dot(xb, win_ref[:, D_FF + lo:D_FF + lo + FFN_CHUNK])
            act_ref[:, lo:lo + FFN_CHUNK] = (gate * jax.nn.sigmoid(gate) * up).astype(BF16)
            for norm_rows in norms[j]:
                norm_rows(_zero_after(gate), scaled_x)
        if first_step:
            ax_ref[...] = scaled_x(slice(None))

    def finish_previous():
        y = _dot(act_ref[...], wout_ref[...])
        return _row_group_norms(ax_ref, y, 0.5, g_ref, b_ref, o_ref)

    _skewed_steps(FFN_LOAD_STEPS, load_weights, start_current, finish_previous)


def _tile_specs(rows, n_load, n_tiles):
    clamp = lambda t: jnp.clip(t, 0, n_tiles - 1)
    return (pl.BlockSpec((rows, D_MODEL), lambda i: (clamp(i - n_load), 0)),
            pl.BlockSpec((rows, D_MODEL), lambda i: (clamp(i - n_load - 1), 0)))


def _ffn_call(x, w_in, w_out, ln_g, ln_b):
    n = x.shape[0]
    n_tiles = n // FFN_TOKENS
    last_chunk = FFN_LOAD_STEPS - 1
    tok_in, tok_out = _tile_specs(FFN_TOKENS, FFN_LOAD_STEPS, n_tiles)
    resident = functools.partial(pl.BlockSpec, index_map=lambda i: (0, 0), pipeline_mode=pl.Buffered(1))
    return pl.pallas_call(
        _ffn_kernel,
        out_shape=jax.ShapeDtypeStruct((n, D_MODEL), F32),
        grid=(FFN_LOAD_STEPS + n_tiles + 1,),
        in_specs=[
            tok_in,
            pl.BlockSpec((D_MODEL, 2 * D_FF // FFN_LOAD_STEPS), lambda i: (0, jnp.minimum(i, last_chunk))),
            pl.BlockSpec((D_FF // FFN_LOAD_STEPS, D_MODEL), lambda i: (jnp.minimum(i, last_chunk), 0)),
            resident((1, D_MODEL)),
            resident((1, D_MODEL)),
        ],
        out_specs=tok_out,
        scratch_shapes=[
            pltpu.VMEM((D_MODEL, 2 * D_FF), BF16),
            pltpu.VMEM((D_FF, D_MODEL), BF16),
            pltpu.VMEM((FFN_TOKENS, D_FF), BF16),
            pltpu.VMEM((FFN_TOKENS, D_MODEL), F32),
        ],
        compiler_params=pltpu.CompilerParams(
            dimension_semantics=("arbitrary",), vmem_limit_bytes=VMEM_LIMIT_BYTES),
        name="ffn_ln",
    )(x, w_in, w_out, ln_g, ln_b)


def _rope(t, cos, sin_signed):
    return t * cos + pltpu.roll(t, HEAD_DIM // 2, 1) * sin_signed


def _gelu(t):
    return 0.5 * t * (1.0 + lax.erf(t * (0.5 ** 0.5)))


def _mixer_kernel(x_ref, win32_ref, wout32_ref, ropeq_ref, ropek_ref, decay_ref, xi_ref, zeta_ref,
                  gn_g_ref, gn_b_ref, zln_g_ref, zln_b_ref, ws_ref, bs_ref, ln_g_ref, ln_b_ref,
                  o_ref, win_ref, wout_ref, state_ref, mix_ref, ax_ref, *, gamma_block, n_seq):
    tile = pl.program_id(0) - MIX_LOAD_STEPS

    def load_weights(step):
        _cast_chunk(step, win32_ref, win_ref, 1, MIX_LOAD_STEPS)
        _cast_chunk(step, wout32_ref, wout_ref, 0, MIX_OUT_CHUNKS)

    def start_current(first_step, finish):
        _mix_tile(x_ref, win_ref, ropeq_ref, ropek_ref, decay_ref, xi_ref, zeta_ref,
                  gn_g_ref, gn_b_ref, zln_g_ref, zln_b_ref, ws_ref, bs_ref, state_ref,
                  mix_ref, ax_ref, finish,
                  first_step=first_step, first_of_sequence=tile % n_seq == 0, gamma_block=gamma_block)

    def finish_previous():
        y = _dot(mix_ref[...], wout_ref[...])
        return _row_group_norms(ax_ref, y, 1.0, ln_g_ref, ln_b_ref, o_ref)

    _skewed_steps(MIX_LOAD_STEPS, load_weights, start_current, finish_previous)


def _mix_tile(x_ref, win_ref, ropeq_ref, ropek_ref, decay_ref, xi_ref, zeta_ref,
              gn_g_ref, gn_b_ref, zln_g_ref, zln_b_ref, ws_ref, bs_ref, state_ref, mix_ref, ax_ref, finish,
              *, first_step, first_of_sequence, gamma_block):
    xb = x_ref[...].astype(BF16)

    def proj(k):
        return _dot(xb, win_ref[:, k * RET_WIDTH:(k + 1) * RET_WIDTH])

    def scaled_x(rows):
        return DEEPNORM_ALPHA * x_ref[rows, :]

    def retention_head(h, q, k, v, g):
        cols = slice(h * HEAD_DIM, (h + 1) * HEAD_DIM)
        if first_step:
            state = jnp.zeros((HEAD_DIM, HEAD_DIM), F32)
        else:
            state = state_ref[h]
            state = jnp.where(first_of_sequence, jnp.zeros_like(state), state)
        for c in range(MIX_TOKENS // RET_BLOCK):
            rows = slice(c * RET_BLOCK, (c + 1) * RET_BLOCK)
            qc = _rope(q[rows, cols], ropeq_ref[rows, :HEAD_DIM], ropeq_ref[rows, HEAD_DIM:]).astype(BF16)
            kc = _rope(k[rows, cols], ropek_ref[rows, :HEAD_DIM], ropek_ref[rows, HEAD_DIM:]).astype(BF16)
            vc = v[rows, cols]
            scores = lax.dot_general(qc, kc, (((1,), (1,)), ((), ())), preferred_element_type=F32)
            scores = scores * decay_ref[h]
            intra = _dot(scores.astype(BF16), vc.astype(BF16))
            cross = _dot(qc, state.astype(BF16)) * xi_ref[h]
            kv = lax.dot_general(kc, (vc * zeta_ref[h]).astype(BF16), (((0,), (0,)), ((), ())),
                                 preferred_element_type=F32)
            state = gamma_block[h] * state + kv
            ret = _layer_norm(intra + cross, gn_g_ref[:, cols], gn_b_ref[:, cols])
            gate = g[rows, cols]
            mix_ref[rows, cols] = (gate * jax.nn.sigmoid(gate) * ret).astype(BF16)
        state_ref[h] = state

    n_sub = MIX_TOKENS // GMLP_CHUNK
    row_id = lax.broadcasted_iota(jnp.int32, (GMLP_CHUNK, GMLP_CHUNK), 0)
    col_id = lax.broadcasted_iota(jnp.int32, (GMLP_CHUNK, GMLP_CHUNK), 1)
    causal = row_id >= col_id

    def gating_group(gi, u, z):
        cols = slice(gi * HEAD_DIM, (gi + 1) * HEAD_DIM)
        zn = _layer_norm(_gelu(z[:, cols]), zln_g_ref[:, cols], zln_b_ref[:, cols]).astype(BF16)
        zcat = jnp.concatenate([zn[s * GMLP_CHUNK:(s + 1) * GMLP_CHUNK] for s in range(n_sub)], axis=1)
        w = jnp.where(causal, ws_ref[gi], 0.0).astype(BF16)
        mixed = _dot(w, zcat)
        for s in range(n_sub):
            rows = slice(s * GMLP_CHUNK, (s + 1) * GMLP_CHUNK)
            m = mixed[:, s * GMLP_CHUNK:(s + 1) * GMLP_CHUNK] + bs_ref[gi]
            mix_ref[rows, RET_WIDTH + gi * HEAD_DIM:RET_WIDTH + (gi + 1) * HEAD_DIM] = (
                _gelu(u[rows, cols]) * m).astype(BF16)

    q, k, v = proj(0), proj(1), proj(2)
    for norm_rows in finish():
        norm_rows(None, scaled_x)
    g = proj(3)
    retention_head(0, q, k, v, g)
    z = proj(5)
    retention_head(1, q, k, v, g)
    u = proj(4)
    retention_head(2, q, k, v, g)
    gating_group(0, u, z)
    gating_group(1, u, z)
    retention_head(3, q, k, v, g)
    gating_group(2, u, z)
    gating_group(3, u, z)
    if first_step:
        ax_ref[...] = scaled_x(slice(None))


def _retention_tables(block):
    log_gamma = np.log1p(-(2.0 ** (-5.0 - np.arange(RET_HEADS, dtype=np.float64))))
    idx = np.arange(block, dtype=np.float64)
    rel = idx[:, None] - idx[None, :]
    decay = np.where(rel[None] >= 0, np.exp(np.maximum(rel, 0.0)[None] * log_gamma[:, None, None]), 0.0)
    xi = np.exp((idx + 1.0)[None, :] * log_gamma[:, None])
    zeta = np.exp((block - 1.0 - idx)[None, :] * log_gamma[:, None])
    widen = lambda t: np.broadcast_to(t[:, :, None], (RET_HEADS, block, HEAD_DIM))
    return tuple(jnp.asarray(t, F32) for t in (decay, widen(xi), widen(zeta)))


def _rope_tables(seq):
    inv_freq = ROPE_BASE ** (-np.arange(0, HEAD_DIM, 2, dtype=np.float64) / HEAD_DIM)
    ang = np.arange(seq, dtype=np.float64)[:, None] * inv_freq[None, :]
    cos, sin = np.cos(ang), np.sin(ang)
    table = np.concatenate([cos, cos, -sin, sin], axis=-1)
    return jnp.asarray(table, F32), jnp.asarray(table * (HEAD_DIM ** -0.5), F32)


def _mixer_call(x, batch, seq, w_in, w_out, gn_g, gn_b, zln_g, zln_b, w_s, b_s, ln_g, ln_b):
    n_seq = seq // MIX_TOKENS
    n_tiles = batch * n_seq
    rope_q, rope_k = _rope_tables(seq)
    decay, xi, zeta = _retention_tables(RET_BLOCK)
    gamma_block = tuple(float((1.0 - 2.0 ** (-5.0 - h)) ** RET_BLOCK) for h in range(RET_HEADS))
    bs_full = jnp.broadcast_to(b_s[:, :, None], (GMLP_GROUPS, GMLP_CHUNK, HEAD_DIM))

    def resident(shape):
        zeros = (0,) * len(shape)
        return pl.BlockSpec(shape, lambda i: zeros, pipeline_mode=pl.Buffered(1))

    tok_in, tok_out = _tile_specs(MIX_TOKENS, MIX_LOAD_STEPS, n_tiles)
    pos = pl.BlockSpec((MIX_TOKENS, 2 * HEAD_DIM),
                       lambda i: (jnp.clip(i - MIX_LOAD_STEPS, 0, n_tiles - 1) % n_seq, 0))
    return pl.pallas_call(
        functools.partial(_mixer_kernel, gamma_block=gamma_block, n_seq=n_seq),
        out_shape=jax.ShapeDtypeStruct(x.shape, F32),
        grid=(MIX_LOAD_STEPS + n_tiles + 1,),
        in_specs=[
            tok_in,
            pl.BlockSpec((D_MODEL, IN_WIDTH // MIX_LOAD_STEPS), lambda i: (0, jnp.minimum(i, MIX_LOAD_STEPS - 1))),
            pl.BlockSpec((D_MODEL // MIX_OUT_CHUNKS, D_MODEL), lambda i: (jnp.minimum(i, MIX_OUT_CHUNKS - 1), 0)),
            pos, pos,
            resident((RET_HEADS, RET_BLOCK, RET_BLOCK)),
            resident((RET_HEADS, RET_BLOCK, HEAD_DIM)),
            resident((RET_HEADS, RET_BLOCK, HEAD_DIM)),
            resident((1, RET_WIDTH)), resident((1, RET_WIDTH)),
            resident((1, GMLP_WIDTH)), resident((1, GMLP_WIDTH)),
            resident((GMLP_GROUPS, GMLP_CHUNK, GMLP_CHUNK)),
            resident((GMLP_GROUPS, GMLP_CHUNK, HEAD_DIM)),
            resident((1, D_MODEL)), resident((1, D_MODEL)),
        ],
        out_specs=tok_out,
        scratch_shapes=[
            pltpu.VMEM((D_MODEL, IN_WIDTH), BF16),
            pltpu.VMEM((D_MODEL, D_MODEL), BF16),
            pltpu.VMEM((RET_HEADS, HEAD_DIM, HEAD_DIM), F32),
            pltpu.VMEM((MIX_TOKENS, D_MODEL), BF16),
            pltpu.VMEM((MIX_TOKENS, D_MODEL), F32),
        ],
        compiler_params=pltpu.CompilerParams(
            dimension_semantics=("arbitrary",), vmem_limit_bytes=VMEM_LIMIT_BYTES),
        name="mixer_ln",
    )(x, w_in, w_out, rope_q, rope_k, decay, xi, zeta, gn_g, gn_b, zln_g, zln_b, w_s, bs_full, ln_g, ln_b)


def kernel(x, ffn1_w_in, ffn1_w_out, ln1_g, ln1_b, mix_w_in, ret_gn_g, ret_gn_b, gmlp_ln_g, gmlp_ln_b,
           gmlp_w_s, gmlp_b_s, mix_w_out, ln2_g, ln2_b, ffn2_w_in, ffn2_w_out, ln3_g, ln3_b):
    batch, seq, d = x.shape
    assert d == D_MODEL and seq % MIX_TOKENS == 0 and (batch * seq) % FFN_TOKENS == 0
    assert MIX_TOKENS % RET_BLOCK == 0 and RET_BLOCK % CHUNK == 0 and MIX_TOKENS % GMLP_CHUNK == 0
    assert 2 * D_FF % FFN_LOAD_STEPS == 0 and IN_WIDTH % MIX_LOAD_STEPS == 0 and D_MODEL % MIX_OUT_CHUNKS == 0
    row = lambda t: t.reshape(1, -1)
    h = x.reshape(batch * seq, d)
    for l in range(DEPTH):
        h = _ffn_call(h, ffn1_w_in[l], ffn1_w_out[l], row(ln1_g[l]), row(ln1_b[l]))
        h = _mixer_call(h, batch, seq, mix_w_in[l], mix_w_out[l],
                        row(ret_gn_g[l]), row(ret_gn_b[l]), row(gmlp_ln_g[l]), row(gmlp_ln_b[l]),
                        gmlp_w_s[l], gmlp_b_s[l], row(ln2_g[l]), row(ln2_b[l]))
        h = _ffn_call(h, ffn2_w_in[l], ffn2_w_out[l], row(ln3_g[l]), row(ln3_b[l]))
    return h.reshape(batch, seq, d)
```

```python
import functools

import jax
import jax.numpy as jnp
import numpy as np
from jax import lax
from jax.experimental import pallas as pl
from jax.experimental.pallas import tpu as pltpu

F32 = jnp.float32
BF16 = jnp.bfloat16

D_MODEL = 1024
DEPTH = 1
CHUNK = 64
RET_HEADS = 4
HEAD_DIM = 128
RET_WIDTH = RET_HEADS * HEAD_DIM
GMLP_GROUPS = 4
GMLP_WIDTH = GMLP_GROUPS * HEAD_DIM
GMLP_CHUNK = 128
IN_WIDTH = 4 * RET_WIDTH + 2 * GMLP_WIDTH
D_FF = 2816
ROPE_BASE = 10000.0
LN_EPS = 1e-5
DEEPNORM_ALPHA = (2.0 * DEPTH) ** 0.25

LANES = 128
SUBLANES = 8
MXU_DIM = 256
VMEM_LIMIT_BYTES = 56 * 1024 * 1024

FFN_TOKENS = 512
FFN_CHUNK = MXU_DIM
MIX_TOKENS = 512
RET_BLOCK = 256
NORM_ROWS = SUBLANES
FFN_OUT_ROWS = 256


def _layer_norm(r, g, b):
    mu = jnp.mean(r, axis=-1, keepdims=True)
    d = r - mu
    var = jnp.mean(d * d, axis=-1, keepdims=True)
    return d * lax.rsqrt(var + LN_EPS) * g + b


def _dot(a, b):
    return jnp.dot(a, b, preferred_element_type=F32)


def _skewed_steps(start_current, finish_previous):
    i = pl.program_id(0)
    last = pl.num_programs(0) - 1

    @pl.when(i == 0)
    def _():
        start_current(True, lambda: [])

    @pl.when(jnp.logical_and(i > 0, i < last))
    def _():
        start_current(False, finish_previous)

    @pl.when(i == last)
    def _():
        for norm_rows in finish_previous():
            norm_rows()


def _row_group_norms(resid_ref, y, g_ref, b_ref, o_ref):
    def norm_rows(rows, next_resid=None):
        o_ref[rows, :] = _layer_norm(resid_ref[rows, :] + y[rows, :], g_ref[...], b_ref[...])
        if next_resid is not None:
            resid_ref[rows, :] = next_resid(rows)
    return [functools.partial(norm_rows, slice(lo, lo + NORM_ROWS)) for lo in range(0, y.shape[0], NORM_ROWS)]


def _ffn_kernel(x_ref, win_ref, wout_ref, g_ref, b_ref, o_ref, act_ref):
    xb = x_ref[...].astype(BF16)
    for j in range(D_FF // FFN_CHUNK):
        lo = j * FFN_CHUNK
        gate = _dot(xb, win_ref[:, lo:lo + FFN_CHUNK])
        up = _dot(xb, win_ref[:, D_FF + lo:D_FF + lo + FFN_CHUNK])
        act_ref[:, lo:lo + FFN_CHUNK] = (gate * jax.nn.sigmoid(gate) * up).astype(BF16)
    for lo in range(0, FFN_TOKENS, FFN_OUT_ROWS):
        rows = slice(lo, lo + FFN_OUT_ROWS)
        y = _dot(act_ref[rows, :], wout_ref[...])
        o_ref[rows, :] = _layer_norm(DEEPNORM_ALPHA * x_ref[rows, :] + 0.5 * y, g_ref[...], b_ref[...])


def _ffn_call(x, w_in, w_out, ln_g, ln_b):
    n = x.shape[0]
    const = lambda i: (0, 0)
    resident = functools.partial(pl.BlockSpec, index_map=const, pipeline_mode=pl.Buffered(1))
    return pl.pallas_call(
        _ffn_kernel,
        out_shape=jax.ShapeDtypeStruct((n, D_MODEL), F32),
        grid=(n // FFN_TOKENS,),
        in_specs=[
            pl.BlockSpec((FFN_TOKENS, D_MODEL), lambda i: (i, 0)),
            resident((D_MODEL, 2 * D_FF)),
            resident((D_FF, D_MODEL)),
            resident((1, D_MODEL)),
            resident((1, D_MODEL)),
        ],
        out_specs=pl.BlockSpec((FFN_TOKENS, D_MODEL), lambda i: (i, 0)),
        scratch_shapes=[pltpu.VMEM((FFN_TOKENS, D_FF), BF16)],
        compiler_params=pltpu.CompilerParams(
            dimension_semantics=("arbitrary",), vmem_limit_bytes=VMEM_LIMIT_BYTES),
        name="ffn_ln",
    )(x, w_in, w_out, ln_g, ln_b)


def _rope(t, cos, sin_signed):
    return t * cos + pltpu.roll(t, HEAD_DIM // 2, 1) * sin_signed


def _gelu(t):
    return 0.5 * t * (1.0 + lax.erf(t * (0.5 ** 0.5)))


def _mixer_kernel(x_ref, win_ref, wout_ref, ropeq_ref, ropek_ref, decay_ref, xi_ref, zeta_ref,
                  gn_g_ref, gn_b_ref, zln_g_ref, zln_b_ref, ws_ref, bs_ref, ln_g_ref, ln_b_ref,
                  o_ref, state_ref, mix_ref, ax_ref, *, gamma_block, n_seq):
    i = pl.program_id(0)

    def start_current(first_step, finish):
        _mix_tile(x_ref, win_ref, ropeq_ref, ropek_ref, decay_ref, xi_ref, zeta_ref,
                  gn_g_ref, gn_b_ref, zln_g_ref, zln_b_ref, ws_ref, bs_ref, state_ref,
                  mix_ref, ax_ref, finish,
                  first_step=first_step, first_of_sequence=i % n_seq == 0, gamma_block=gamma_block)

    def finish_previous():
        y = _dot(mix_ref[...], wout_ref[...])
        return _row_group_norms(ax_ref, y, ln_g_ref, ln_b_ref, o_ref)

    _skewed_steps(start_current, finish_previous)


def _mix_tile(x_ref, win_ref, ropeq_ref, ropek_ref, decay_ref, xi_ref, zeta_ref,
              gn_g_ref, gn_b_ref, zln_g_ref, zln_b_ref, ws_ref, bs_ref, state_ref, mix_ref, ax_ref, finish,
              *, first_step, first_of_sequence, gamma_block):
    xb = x_ref[...].astype(BF16)

    def proj(k):
        return _dot(xb, win_ref[:, k * RET_WIDTH:(k + 1) * RET_WIDTH])

    def scaled_x(rows):
        return DEEPNORM_ALPHA * x_ref[rows, :]

    def retention_head(h, q, k, v, g):
        cols = slice(h * HEAD_DIM, (h + 1) * HEAD_DIM)
        if first_step:
            state = jnp.zeros((HEAD_DIM, HEAD_DIM), F32)
        else:
            state = state_ref[h]
            state = jnp.where(first_of_sequence, jnp.zeros_like(state), state)
        for c in range(MIX_TOKENS // RET_BLOCK):
            rows = slice(c * RET_BLOCK, (c + 1) * RET_BLOCK)
            qc = _rope(q[rows, cols], ropeq_ref[rows, :HEAD_DIM], ropeq_ref[rows, HEAD_DIM:]).astype(BF16)
            kc = _rope(k[rows, cols], ropek_ref[rows, :HEAD_DIM], ropek_ref[rows, HEAD_DIM:]).astype(BF16)
            vc = v[rows, cols]
            scores = lax.dot_general(qc, kc, (((1,), (1,)), ((), ())), preferred_element_type=F32)
            scores = scores * decay_ref[h]
            intra = _dot(scores.astype(BF16), vc.astype(BF16))
            cross = _dot(qc, state.astype(BF16)) * xi_ref[h]
            kv = lax.dot_general(kc, (vc * zeta_ref[h]).astype(BF16), (((0,), (0,)), ((), ())),
                                 preferred_element_type=F32)
            state = gamma_block[h] * state + kv
            ret = _layer_norm(intra + cross, gn_g_ref[:, cols], gn_b_ref[:, cols])
            gate = g[rows, cols]
            mix_ref[rows, cols] = (gate * jax.nn.sigmoid(gate) * ret).astype(BF16)
        state_ref[h] = state

    n_sub = MIX_TOKENS // GMLP_CHUNK
    row_id = lax.broadcasted_iota(jnp.int32, (GMLP_CHUNK, GMLP_CHUNK), 0)
    col_id = lax.broadcasted_iota(jnp.int32, (GMLP_CHUNK, GMLP_CHUNK), 1)
    causal = row_id >= col_id

    def gating_group(gi, u, z):
        cols = slice(gi * HEAD_DIM, (gi + 1) * HEAD_DIM)
        zn = _layer_norm(_gelu(z[:, cols]), zln_g_ref[:, cols], zln_b_ref[:, cols]).astype(BF16)
        zcat = jnp.concatenate([zn[s * GMLP_CHUNK:(s + 1) * GMLP_CHUNK] for s in range(n_sub)], axis=1)
        w = jnp.where(causal, ws_ref[gi], 0.0).astype(BF16)
        mixed = _dot(w, zcat)
        for s in range(n_sub):
            rows = slice(s * GMLP_CHUNK, (s + 1) * GMLP_CHUNK)
            m = mixed[:, s * GMLP_CHUNK:(s + 1) * GMLP_CHUNK] + bs_ref[gi]
            mix_ref[rows, RET_WIDTH + gi * HEAD_DIM:RET_WIDTH + (gi + 1) * HEAD_DIM] = (
                _gelu(u[rows, cols]) * m).astype(BF16)

    q, k, v = proj(0), proj(1), proj(2)
    for norm_rows in finish():
        norm_rows(scaled_x)
    g = proj(3)
    retention_head(0, q, k, v, g)
    z = proj(5)
    retention_head(1, q, k, v, g)
    u = proj(4)
    retention_head(2, q, k, v, g)
    gating_group(0, u, z)
    gating_group(1, u, z)
    retention_head(3, q, k, v, g)
    gating_group(2, u, z)
    gating_group(3, u, z)
    if first_step:
        ax_ref[...] = scaled_x(slice(None))


def _retention_tables(block):
    log_gamma = np.log1p(-(2.0 ** (-5.0 - np.arange(RET_HEADS, dtype=np.float64))))
    idx = np.arange(block, dtype=np.float64)
    rel = idx[:, None] - idx[None, :]
    decay = np.where(rel[None] >= 0, np.exp(np.maximum(rel, 0.0)[None] * log_gamma[:, None, None]), 0.0)
    xi = np.exp((idx + 1.0)[None, :] * log_gamma[:, None])
    zeta = np.exp((block - 1.0 - idx)[None, :] * log_gamma[:, None])
    widen = lambda t: np.broadcast_to(t[:, :, None], (RET_HEADS, block, HEAD_DIM))
    return tuple(jnp.asarray(t, F32) for t in (decay, widen(xi), widen(zeta)))


def _rope_tables(seq):
    inv_freq = ROPE_BASE ** (-np.arange(0, HEAD_DIM, 2, dtype=np.float64) / HEAD_DIM)
    ang = np.arange(seq, dtype=np.float64)[:, None] * inv_freq[None, :]
    cos, sin = np.cos(ang), np.sin(ang)
    table = np.concatenate([cos, cos, -sin, sin], axis=-1)
    return jnp.asarray(table, F32), jnp.asarray(table * (HEAD_DIM ** -0.5), F32)


def _mixer_call(x, batch, seq, w_in, w_out, gn_g, gn_b, zln_g, zln_b, w_s, b_s, ln_g, ln_b):
    n_seq = seq // MIX_TOKENS
    n_tiles = batch * n_seq
    rope_q, rope_k = _rope_tables(seq)
    decay, xi, zeta = _retention_tables(RET_BLOCK)
    gamma_block = tuple(float((1.0 - 2.0 ** (-5.0 - h)) ** RET_BLOCK) for h in range(RET_HEADS))
    bs_full = jnp.broadcast_to(b_s[:, :, None], (GMLP_GROUPS, GMLP_CHUNK, HEAD_DIM))

    def resident(shape):
        zeros = (0,) * len(shape)
        return pl.BlockSpec(shape, lambda i: zeros, pipeline_mode=pl.Buffered(1))

    tok_in = pl.BlockSpec((MIX_TOKENS, D_MODEL), lambda i: (jnp.minimum(i, n_tiles - 1), 0))
    tok_out = pl.BlockSpec((MIX_TOKENS, D_MODEL), lambda i: (jnp.maximum(i - 1, 0), 0))
    pos = pl.BlockSpec((MIX_TOKENS, 2 * HEAD_DIM), lambda i: (jnp.minimum(i, n_tiles - 1) % n_seq, 0))
    return pl.pallas_call(
        functools.partial(_mixer_kernel, gamma_block=gamma_block, n_seq=n_seq),
        out_shape=jax.ShapeDtypeStruct(x.shape, F32),
        grid=(n_tiles + 1,),
        in_specs=[
            tok_in,
            resident((D_MODEL, IN_WIDTH)),
            resident((D_MODEL, D_MODEL)),
            pos, pos,
            resident((RET_HEADS, RET_BLOCK, RET_BLOCK)),
            resident((RET_HEADS, RET_BLOCK, HEAD_DIM)),
            resident((RET_HEADS, RET_BLOCK, HEAD_DIM)),
            resident((1, RET_WIDTH)), resident((1, RET_WIDTH)),
            resident((1, GMLP_WIDTH)), resident((1, GMLP_WIDTH)),
            resident((GMLP_GROUPS, GMLP_CHUNK, GMLP_CHUNK)),
            resident((GMLP_GROUPS, GMLP_CHUNK, HEAD_DIM)),
            resident((1, D_MODEL)), resident((1, D_MODEL)),
        ],
        out_specs=tok_out,
        scratch_shapes=[
            pltpu.VMEM((RET_HEADS, HEAD_DIM, HEAD_DIM), F32),
            pltpu.VMEM((MIX_TOKENS, D_MODEL), BF16),
            pltpu.VMEM((MIX_TOKENS, D_MODEL), F32),
        ],
        compiler_params=pltpu.CompilerParams(
            dimension_semantics=("arbitrary",), vmem_limit_bytes=VMEM_LIMIT_BYTES),
        name="mixer_ln",
    )(x, w_in, w_out, rope_q, rope_k, decay, xi, zeta, gn_g, gn_b, zln_g, zln_b, w_s, bs_full, ln_g, ln_b)


def kernel(x, ffn1_w_in, ffn1_w_out, ln1_g, ln1_b, mix_w_in, ret_gn_g, ret_gn_b, gmlp_ln_g, gmlp_ln_b,
           gmlp_w_s, gmlp_b_s, mix_w_out, ln2_g, ln2_b, ffn2_w_in, ffn2_w_out, ln3_g, ln3_b):
    batch, seq, d = x.shape
    assert d == D_MODEL and seq % MIX_TOKENS == 0 and (batch * seq) % FFN_TOKENS == 0
    assert MIX_TOKENS % RET_BLOCK == 0 and RET_BLOCK % CHUNK == 0 and MIX_TOKENS % GMLP_CHUNK == 0
    row = lambda t: t.reshape(1, -1)
    h = x.reshape(batch * seq, d)
    for l in range(DEPTH):
        h = _ffn_call(h, ffn1_w_in[l].astype(BF16), ffn1_w_out[l].astype(BF16), row(ln1_g[l]), row(ln1_b[l]))
        h = _mixer_call(h, batch, seq, mix_w_in[l].astype(BF16), mix_w_out[l].astype(BF16),
                        row(ret_gn_g[l]), row(ret_gn_b[l]), row(gmlp_ln_g[l]), row(gmlp_ln_b[l]),
                        gmlp_w_s[l], gmlp_b_s[l], row(ln2_g[l]), row(ln2_b[l]))
        h = _ffn_call(h, ffn2_w_in[l].astype(BF16), ffn2_w_out[l].astype(BF16), row(ln3_g[l]), row(ln3_b[l]))
    return h.reshape(batch, seq, d)
```

```python
import functools

import jax
import jax.numpy as jnp
import numpy as np
from jax import lax
from jax.experimental import pallas as pl
from jax.experimental.pallas import tpu as pltpu

F32 = jnp.float32
BF16 = jnp.bfloat16

D_MODEL = 1024
DEPTH = 1
CHUNK = 64
RET_HEADS = 4
HEAD_DIM = 128
RET_WIDTH = RET_HEADS * HEAD_DIM
GMLP_GROUPS = 4
GMLP_WIDTH = GMLP_GROUPS * HEAD_DIM
GMLP_CHUNK = 128
IN_WIDTH = 4 * RET_WIDTH + 2 * GMLP_WIDTH
D_FF = 2816
ROPE_BASE = 10000.0
LN_EPS = 1e-5
DEEPNORM_ALPHA = (2.0 * DEPTH) ** 0.25

LANES = 128
SUBLANES = 8
MXU_DIM = 256
VMEM_LIMIT_BYTES = 56 * 1024 * 1024

FFN_TOKENS = 1024
FFN_CHUNK = MXU_DIM
MIX_TOKENS = 512
RET_BLOCK = 256
NORM_ROWS = SUBLANES
FFN_OUT_SPLITS = (256, 512, 768)


def _layer_norm(r, g, b):
    mu = jnp.mean(r, axis=-1, keepdims=True)
    d = r - mu
    var = jnp.mean(d * d, axis=-1, keepdims=True)
    return d * lax.rsqrt(var + LN_EPS) * g + b


def _dot(a, b):
    return jnp.dot(a, b, preferred_element_type=F32)


def _skewed_steps(start_current, finish_previous):
    i = pl.program_id(0)
    last = pl.num_programs(0) - 1

    @pl.when(i == 0)
    def _():
        start_current(True, lambda: [])

    @pl.when(jnp.logical_and(i > 0, i < last))
    def _():
        start_current(False, finish_previous)

    @pl.when(i == last)
    def _():
        for norm_rows in finish_previous():
            norm_rows()


def _row_group_norms(resid_ref, y, g_ref, b_ref, o_ref):
    def norm_rows(rows, next_resid=None):
        o_ref[rows, :] = _layer_norm(resid_ref[rows, :] + y[rows, :], g_ref[...], b_ref[...])
        if next_resid is not None:
            resid_ref[rows, :] = next_resid(rows)
    return [functools.partial(norm_rows, slice(lo, lo + NORM_ROWS)) for lo in range(0, y.shape[0], NORM_ROWS)]


def _ffn_kernel(x_ref, win_ref, wout_ref, g_ref, b_ref, o_ref, act_ref):
    xb = x_ref[...].astype(BF16)
    for j in range(D_FF // FFN_CHUNK):
        lo = j * FFN_CHUNK
        gate = _dot(xb, win_ref[:, lo:lo + FFN_CHUNK])
        up = _dot(xb, win_ref[:, D_FF + lo:D_FF + lo + FFN_CHUNK])
        act_ref[:, lo:lo + FFN_CHUNK] = (gate * jax.nn.sigmoid(gate) * up).astype(BF16)
    for lo, hi in zip((0,) + FFN_OUT_SPLITS, FFN_OUT_SPLITS + (FFN_TOKENS,)):
        rows = slice(lo, hi)
        y = _dot(act_ref[rows, :], wout_ref[...])
        o_ref[rows, :] = _layer_norm(DEEPNORM_ALPHA * x_ref[rows, :] + 0.5 * y, g_ref[...], b_ref[...])


def _ffn_call(x, w_in, w_out, ln_g, ln_b):
    n = x.shape[0]
    const = lambda i: (0, 0)
    resident = functools.partial(pl.BlockSpec, index_map=const, pipeline_mode=pl.Buffered(1))
    return pl.pallas_call(
        _ffn_kernel,
        out_shape=jax.ShapeDtypeStruct((n, D_MODEL), F32),
        grid=(n // FFN_TOKENS,),
        in_specs=[
            pl.BlockSpec((FFN_TOKENS, D_MODEL), lambda i: (i, 0)),
            resident((D_MODEL, 2 * D_FF)),
            resident((D_FF, D_MODEL)),
            resident((1, D_MODEL)),
            resident((1, D_MODEL)),
        ],
        out_specs=pl.BlockSpec((FFN_TOKENS, D_MODEL), lambda i: (i, 0)),
        scratch_shapes=[pltpu.VMEM((FFN_TOKENS, D_FF), BF16)],
        compiler_params=pltpu.CompilerParams(
            dimension_semantics=("arbitrary",), vmem_limit_bytes=VMEM_LIMIT_BYTES),
        name="ffn_ln",
    )(x, w_in, w_out, ln_g, ln_b)


def _rope(t, cos, sin_signed):
    return t * cos + pltpu.roll(t, HEAD_DIM // 2, 1) * sin_signed


def _gelu(t):
    return 0.5 * t * (1.0 + lax.erf(t * (0.5 ** 0.5)))


def _mixer_kernel(x_ref, win_ref, wout_ref, ropeq_ref, ropek_ref, decay_ref, xi_ref, zeta_ref,
                  gn_g_ref, gn_b_ref, zln_g_ref, zln_b_ref, ws_ref, bs_ref, ln_g_ref, ln_b_ref,
                  o_ref, state_ref, mix_ref, ax_ref, *, gamma_block, n_seq):
    i = pl.program_id(0)

    def start_current(first_step, finish):
        _mix_tile(x_ref, win_ref, ropeq_ref, ropek_ref, decay_ref, xi_ref, zeta_ref,
                  gn_g_ref, gn_b_ref, zln_g_ref, zln_b_ref, ws_ref, bs_ref, state_ref,
                  mix_ref, ax_ref, finish,
                  first_step=first_step, first_of_sequence=i % n_seq == 0, gamma_block=gamma_block)

    def finish_previous():
        y = _dot(mix_ref[...], wout_ref[...])
        return _row_group_norms(ax_ref, y, ln_g_ref, ln_b_ref, o_ref)

    _skewed_steps(start_current, finish_previous)


def _mix_tile(x_ref, win_ref, ropeq_ref, ropek_ref, decay_ref, xi_ref, zeta_ref,
              gn_g_ref, gn_b_ref, zln_g_ref, zln_b_ref, ws_ref, bs_ref, state_ref, mix_ref, ax_ref, finish,
              *, first_step, first_of_sequence, gamma_block):
    xb = x_ref[...].astype(BF16)

    def proj(k):
        return _dot(xb, win_ref[:, k * RET_WIDTH:(k + 1) * RET_WIDTH])

    def scaled_x(rows):
        return DEEPNORM_ALPHA * x_ref[rows, :]

    def retention_head(h, q, k, v, g):
        cols = slice(h * HEAD_DIM, (h + 1) * HEAD_DIM)
        if first_step:
            state = jnp.zeros((HEAD_DIM, HEAD_DIM), F32)
        else:
            state = state_ref[h]
            state = jnp.where(first_of_sequence, jnp.zeros_like(state), state)
        for c in range(MIX_TOKENS // RET_BLOCK):
            rows = slice(c * RET_BLOCK, (c + 1) * RET_BLOCK)
            qc = _rope(q[rows, cols], ropeq_ref[rows, :HEAD_DIM], ropeq_ref[rows, HEAD_DIM:]).astype(BF16)
            kc = _rope(k[rows, cols], ropek_ref[rows, :HEAD_DIM], ropek_ref[rows, HEAD_DIM:]).astype(BF16)
            vc = v[rows, cols]
            scores = lax.dot_general(qc, kc, (((1,), (1,)), ((), ())), preferred_element_type=F32)
            scores = scores * decay_ref[h]
            intra = _dot(scores.astype(BF16), vc.astype(BF16))
            cross = _dot(qc, state.astype(BF16)) * xi_ref[h]
            kv = lax.dot_general(kc, (vc * zeta_ref[h]).astype(BF16), (((0,), (0,)), ((), ())),
                                 preferred_element_type=F32)
            state = gamma_block[h] * state + kv
            ret = _layer_norm(intra + cross, gn_g_ref[:, cols], gn_b_ref[:, cols])
            gate = g[rows, cols]
            mix_ref[rows, cols] = (gate * jax.nn.sigmoid(gate) * ret).astype(BF16)
        state_ref[h] = state

    n_sub = MIX_TOKENS // GMLP_CHUNK
    row_id = lax.broadcasted_iota(jnp.int32, (GMLP_CHUNK, GMLP_CHUNK), 0)
    col_id = lax.broadcasted_iota(jnp.int32, (GMLP_CHUNK, GMLP_CHUNK), 1)
    causal = row_id >= col_id

    def gating_group(gi, u, z):
        cols = slice(gi * HEAD_DIM, (gi + 1) * HEAD_DIM)
        zn = _layer_norm(_gelu(z[:, cols]), zln_g_ref[:, cols], zln_b_ref[:, cols]).astype(BF16)
        zcat = jnp.concatenate([zn[s * GMLP_CHUNK:(s + 1) * GMLP_CHUNK] for s in range(n_sub)], axis=1)
        w = jnp.where(causal, ws_ref[gi], 0.0).astype(BF16)
        mixed = _dot(w, zcat)
        for s in range(n_sub):
            rows = slice(s * GMLP_CHUNK, (s + 1) * GMLP_CHUNK)
            m = mixed[:, s * GMLP_CHUNK:(s + 1) * GMLP_CHUNK] + bs_ref[gi]
            mix_ref[rows, RET_WIDTH + gi * HEAD_DIM:RET_WIDTH + (gi + 1) * HEAD_DIM] = (
                _gelu(u[rows, cols]) * m).astype(BF16)

    q, k, v = proj(0), proj(1), proj(2)
    for norm_rows in finish():
        norm_rows(scaled_x)
    g = proj(3)
    retention_head(0, q, k, v, g)
    z = proj(5)
    retention_head(1, q, k, v, g)
    u = proj(4)
    retention_head(2, q, k, v, g)
    gating_group(0, u, z)
    gating_group(1, u, z)
    retention_head(3, q, k, v, g)
    gating_group(2, u, z)
    gating_group(3, u, z)
    if first_step:
        ax_ref[...] = scaled_x(slice(None))


def _retention_tables(block):
    log_gamma = np.log1p(-(2.0 ** (-5.0 - np.arange(RET_HEADS, dtype=np.float64))))
    idx = np.arange(block, dtype=np.float64)
    rel = idx[:, None] - idx[None, :]
    decay = np.where(rel[None] >= 0, np.exp(np.maximum(rel, 0.0)[None] * log_gamma[:, None, None]), 0.0)
    xi = np.exp((idx + 1.0)[None, :] * log_gamma[:, None])
    zeta = np.exp((block - 1.0 - idx)[None, :] * log_gamma[:, None])
    widen = lambda t: np.broadcast_to(t[:, :, None], (RET_HEADS, block, HEAD_DIM))
    return tuple(jnp.asarray(t, F32) for t in (decay, widen(xi), widen(zeta)))


def _rope_tables(seq):
    inv_freq = ROPE_BASE ** (-np.arange(0, HEAD_DIM, 2, dtype=np.float64) / HEAD_DIM)
    ang = np.arange(seq, dtype=np.float64)[:, None] * inv_freq[None, :]
    cos, sin = np.cos(ang), np.sin(ang)
    table = np.concatenate([cos, cos, -sin, sin], axis=-1)
    return jnp.asarray(table, F32), jnp.asarray(table * (HEAD_DIM ** -0.5), F32)


def _mixer_call(x, batch, seq, w_in, w_out, gn_g, gn_b, zln_g, zln_b, w_s, b_s, ln_g, ln_b):
    n_seq = seq // MIX_TOKENS
    n_tiles = batch * n_seq
    rope_q, rope_k = _rope_tables(seq)
    decay, xi, zeta = _retention_tables(RET_BLOCK)
    gamma_block = tuple(float((1.0 - 2.0 ** (-5.0 - h)) ** RET_BLOCK) for h in range(RET_HEADS))
    bs_full = jnp.broadcast_to(b_s[:, :, None], (GMLP_GROUPS, GMLP_CHUNK, HEAD_DIM))

    def resident(shape):
        zeros = (0,) * len(shape)
        return pl.BlockSpec(shape, lambda i: zeros, pipeline_mode=pl.Buffered(1))

    tok_in = pl.BlockSpec((MIX_TOKENS, D_MODEL), lambda i: (jnp.minimum(i, n_tiles - 1), 0))
    tok_out = pl.BlockSpec((MIX_TOKENS, D_MODEL), lambda i: (jnp.maximum(i - 1, 0), 0))
    pos = pl.BlockSpec((MIX_TOKENS, 2 * HEAD_DIM), lambda i: (jnp.minimum(i, n_tiles - 1) % n_seq, 0))
    return pl.pallas_call(
        functools.partial(_mixer_kernel, gamma_block=gamma_block, n_seq=n_seq),
        out_shape=jax.ShapeDtypeStruct(x.shape, F32),
        grid=(n_tiles + 1,),
        in_specs=[
            tok_in,
            resident((D_MODEL, IN_WIDTH)),
            resident((D_MODEL, D_MODEL)),
            pos, pos,
            resident((RET_HEADS, RET_BLOCK, RET_BLOCK)),
            resident((RET_HEADS, RET_BLOCK, HEAD_DIM)),
            resident((RET_HEADS, RET_BLOCK, HEAD_DIM)),
            resident((1, RET_WIDTH)), resident((1, RET_WIDTH)),
            resident((1, GMLP_WIDTH)), resident((1, GMLP_WIDTH)),
            resident((GMLP_GROUPS, GMLP_CHUNK, GMLP_CHUNK)),
            resident((GMLP_GROUPS, GMLP_CHUNK, HEAD_DIM)),
            resident((1, D_MODEL)), resident((1, D_MODEL)),
        ],
        out_specs=tok_out,
        scratch_shapes=[
            pltpu.VMEM((RET_HEADS, HEAD_DIM, HEAD_DIM), F32),
            pltpu.VMEM((MIX_TOKENS, D_MODEL), BF16),
            pltpu.VMEM((MIX_TOKENS, D_MODEL), F32),
        ],
        compiler_params=pltpu.CompilerParams(
            dimension_semantics=("arbitrary",), vmem_limit_bytes=VMEM_LIMIT_BYTES),
        name="mixer_ln",
    )(x, w_in, w_out, rope_q, rope_k, decay, xi, zeta, gn_g, gn_b, zln_g, zln_b, w_s, bs_full, ln_g, ln_b)


def kernel(x, ffn1_w_in, ffn1_w_out, ln1_g, ln1_b, mix_w_in, ret_gn_g, ret_gn_b, gmlp_ln_g, gmlp_ln_b,
           gmlp_w_s, gmlp_b_s, mix_w_out, ln2_g, ln2_b, ffn2_w_in, ffn2_w_out, ln3_g, ln3_b):
    batch, seq, d = x.shape
    assert d == D_MODEL and seq % MIX_TOKENS == 0 and (batch * seq) % FFN_TOKENS == 0
    assert MIX_TOKENS % RET_BLOCK == 0 and RET_BLOCK % CHUNK == 0 and MIX_TOKENS % GMLP_CHUNK == 0
    row = lambda t: t.reshape(1, -1)
    h = x.reshape(batch * seq, d)
    for l in range(DEPTH):
        h = _ffn_call(h, ffn1_w_in[l].astype(BF16), ffn1_w_out[l].astype(BF16), row(ln1_g[l]), row(ln1_b[l]))
        h = _mixer_call(h, batch, seq, mix_w_in[l].astype(BF16), mix_w_out[l].astype(BF16),
                        row(ret_gn_g[l]), row(ret_gn_b[l]), row(gmlp_ln_g[l]), row(gmlp_ln_b[l]),
                        gmlp_w_s[l], gmlp_b_s[l], row(ln2_g[l]), row(ln2_b[l]))
        h = _ffn_call(h, ffn2_w_in[l].astype(BF16), ffn2_w_out[l].astype(BF16), row(ln3_g[l]), row(ln3_b[l]))
    return h.reshape(batch, seq, d)
```

```python
import functools

import jax
import jax.numpy as jnp
import numpy as np
from jax import lax
from jax.experimental import pallas as pl
from jax.experimental.pallas import tpu as pltpu

F32 = jnp.float32
BF16 = jnp.bfloat16

D_MODEL = 1024
DEPTH = 1
CHUNK = 64
RET_HEADS = 4
HEAD_DIM = 128
RET_WIDTH = RET_HEADS * HEAD_DIM
GMLP_GROUPS = 4
GMLP_WIDTH = GMLP_GROUPS * HEAD_DIM
GMLP_CHUNK = 128
IN_WIDTH = 4 * RET_WIDTH + 2 * GMLP_WIDTH
D_FF = 2816
ROPE_BASE = 10000.0
LN_EPS = 1e-5
DEEPNORM_ALPHA = (2.0 * DEPTH) ** 0.25

LANES = 128
SUBLANES = 8
MXU_DIM = 256
VMEM_LIMIT_BYTES = 56 * 1024 * 1024

FFN_TOKENS = 1024
FFN_CHUNK = MXU_DIM
MIX_TOKENS = 512
RET_BLOCK = 256
NORM_ROWS = SUBLANES
FFN_OUT_SPLITS = (256, 512, 768)


def _layer_norm(r, g, b):
    mu = jnp.mean(r, axis=-1, keepdims=True)
    d = r - mu
    var = jnp.mean(d * d, axis=-1, keepdims=True)
    return d * lax.rsqrt(var + LN_EPS) * g + b


def _dot(a, b):
    return jnp.dot(a, b, preferred_element_type=F32)


def _skewed_steps(start_current, finish_previous):
    i = pl.program_id(0)
    last = pl.num_programs(0) - 1

    @pl.when(i == 0)
    def _():
        start_current(True, lambda: [])

    @pl.when(jnp.logical_and(i > 0, i < last))
    def _():
        start_current(False, finish_previous)

    @pl.when(i == last)
    def _():
        for norm_rows in finish_previous():
            norm_rows()


def _row_group_norms(resid_ref, y, g_ref, b_ref, o_ref):
    def norm_rows(rows, next_resid=None):
        o_ref[rows, :] = _layer_norm(resid_ref[rows, :] + y[rows, :], g_ref[...], b_ref[...])
        if next_resid is not None:
            resid_ref[rows, :] = next_resid(rows)
    return [functools.partial(norm_rows, slice(lo, lo + NORM_ROWS)) for lo in range(0, y.shape[0], NORM_ROWS)]


def _ffn_kernel(x_ref, win_ref, wout_ref, g_ref, b_ref, o_ref, act_ref):
    xb = x_ref[...].astype(BF16)
    for j in range(D_FF // FFN_CHUNK):
        lo = j * FFN_CHUNK
        gate = _dot(xb, win_ref[:, lo:lo + FFN_CHUNK])
        up = _dot(xb, win_ref[:, D_FF + lo:D_FF + lo + FFN_CHUNK])
        act_ref[:, lo:lo + FFN_CHUNK] = (gate * jax.nn.sigmoid(gate) * up).astype(BF16)
    for lo, hi in zip((0,) + FFN_OUT_SPLITS, FFN_OUT_SPLITS + (FFN_TOKENS,)):
        rows = slice(lo, hi)
        y = _dot(act_ref[rows, :], wout_ref[...])
        o_ref[rows, :] = _layer_norm(DEEPNORM_ALPHA * x_ref[rows, :] + 0.5 * y, g_ref[...], b_ref[...])


def _ffn_call(x, w_in, w_out, ln_g, ln_b):
    n = x.shape[0]
    const = lambda i: (0, 0)
    resident = functools.partial(pl.BlockSpec, index_map=const, pipeline_mode=pl.Buffered(1))
    return pl.pallas_call(
        _ffn_kernel,
        out_shape=jax.ShapeDtypeStruct((n, D_MODEL), F32),
        grid=(n // FFN_TOKENS,),
        in_specs=[
            pl.BlockSpec((FFN_TOKENS, D_MODEL), lambda i: (i, 0)),
            resident((D_MODEL, 2 * D_FF)),
            resident((D_FF, D_MODEL)),
            resident((1, D_MODEL)),
            resident((1, D_MODEL)),
        ],
        out_specs=pl.BlockSpec((FFN_TOKENS, D_MODEL), lambda i: (i, 0)),
        scratch_shapes=[pltpu.VMEM((FFN_TOKENS, D_FF), BF16)],
        compiler_params=pltpu.CompilerParams(
            dimension_semantics=("arbitrary",), vmem_limit_bytes=VMEM_LIMIT_BYTES),
        name="ffn_ln",
    )(x, w_in, w_out, ln_g, ln_b)


def _rope(t, cos, sin_signed):
    return t * cos + pltpu.roll(t, HEAD_DIM // 2, 1) * sin_signed


def _gelu(t):
    return 0.5 * t * (1.0 + lax.erf(t * (0.5 ** 0.5)))


def _mixer_kernel(x_ref, win_ref, wout_ref, ropeq_ref, ropek_ref, decay_ref, xi_ref, zeta_ref,
                  gn_g_ref, gn_b_ref, zln_g_ref, zln_b_ref, ws_ref, bs_ref, ln_g_ref, ln_b_ref,
                  o_ref, state_ref, mix_ref, ax_ref, *, gamma_block, n_seq):
    i = pl.program_id(0)

    def start_current(first_step, finish):
        _mix_tile(x_ref, win_ref, ropeq_ref, ropek_ref, decay_ref, xi_ref, zeta_ref,
                  gn_g_ref, gn_b_ref, zln_g_ref, zln_b_ref, ws_ref, bs_ref, state_ref,
                  mix_ref, ax_ref, finish,
                  first_step=first_step, first_of_sequence=i % n_seq == 0, gamma_block=gamma_block)

    def finish_previous():
        y = _dot(mix_ref[...], wout_ref[...])
        return _row_group_norms(ax_ref, y, ln_g_ref, ln_b_ref, o_ref)

    _skewed_steps(start_current, finish_previous)


def _mix_tile(x_ref, win_ref, ropeq_ref, ropek_ref, decay_ref, xi_ref, zeta_ref,
              gn_g_ref, gn_b_ref, zln_g_ref, zln_b_ref, ws_ref, bs_ref, state_ref, mix_ref, ax_ref, finish,
              *, first_step, first_of_sequence, gamma_block):
    xb = x_ref[...].astype(BF16)
    blocks = range(MIX_TOKENS // RET_BLOCK)
    block_rows = [slice(c * RET_BLOCK, (c + 1) * RET_BLOCK) for c in blocks]
    half = RET_WIDTH // 2
    contract_last = (((1,), (1,)), ((), ()))
    contract_first = (((0,), (0,)), ((), ()))

    def proj(k):
        return _dot(xb, win_ref[:, k * RET_WIDTH:(k + 1) * RET_WIDTH])

    def proj_half(k, part):
        lo = k * RET_WIDTH + part * half
        return _dot(xb, win_ref[:, lo:lo + half])

    def unit_cols(halves, i):
        return halves[i // 2][:, (i % 2) * HEAD_DIM:(i % 2 + 1) * HEAD_DIM]

    def scaled_x(rows):
        return DEEPNORM_ALPHA * x_ref[rows, :]

    def head_scores(h):
        cols = slice(h * HEAD_DIM, (h + 1) * HEAD_DIM)
        qr, scores, kv = [], [], []
        for rows in block_rows:
            qc = _rope(q[rows, cols], ropeq_ref[rows, :HEAD_DIM], ropeq_ref[rows, HEAD_DIM:]).astype(BF16)
            kc = _rope(k[rows, cols], ropek_ref[rows, :HEAD_DIM], ropek_ref[rows, HEAD_DIM:]).astype(BF16)
            qr.append(qc)
            scores.append(lax.dot_general(qc, kc, contract_last, preferred_element_type=F32))
            kv.append(lax.dot_general(kc, (v[rows, cols] * zeta_ref[h]).astype(BF16), contract_first,
                                      preferred_element_type=F32))
        return qr, scores, kv

    def head_retention(h, qr, scores, kv):
        cols = slice(h * HEAD_DIM, (h + 1) * HEAD_DIM)
        if first_step:
            state = jnp.zeros((HEAD_DIM, HEAD_DIM), F32)
        else:
            state = state_ref[h]
            state = jnp.where(first_of_sequence, jnp.zeros_like(state), state)
        ret = []
        for c in blocks:
            decayed = (scores[c] * decay_ref[h]).astype(BF16)
            intra = _dot(decayed, v[block_rows[c], cols].astype(BF16))
            cross = _dot(qr[c], state.astype(BF16)) * xi_ref[h]
            ret.append(intra + cross)
            state = gamma_block[h] * state + kv[c]
        state_ref[h] = state
        return ret

    def head_gate(h, ret, g_halves):
        cols = slice(h * HEAD_DIM, (h + 1) * HEAD_DIM)
        gate_cols = unit_cols(g_halves, h)
        for c in blocks:
            gate = gate_cols[block_rows[c], :]
            normed = _layer_norm(ret[c], gn_g_ref[:, cols], gn_b_ref[:, cols])
            mix_ref[block_rows[c], cols] = (gate * jax.nn.sigmoid(gate) * normed).astype(BF16)

    n_sub = MIX_TOKENS // GMLP_CHUNK
    row_id = lax.broadcasted_iota(jnp.int32, (GMLP_CHUNK, GMLP_CHUNK), 0)
    col_id = lax.broadcasted_iota(jnp.int32, (GMLP_CHUNK, GMLP_CHUNK), 1)
    causal = row_id >= col_id

    def group_mix(gi, z_halves):
        cols = slice(gi * HEAD_DIM, (gi + 1) * HEAD_DIM)
        zn = _layer_norm(_gelu(unit_cols(z_halves, gi)), zln_g_ref[:, cols], zln_b_ref[:, cols]).astype(BF16)
        zcat = jnp.concatenate([zn[s * GMLP_CHUNK:(s + 1) * GMLP_CHUNK] for s in range(n_sub)], axis=1)
        w = jnp.where(causal, ws_ref[gi], 0.0).astype(BF16)
        return _dot(w, zcat)

    def group_gate(gi, mixed, u_halves):
        u_cols = unit_cols(u_halves, gi)
        for s in range(n_sub):
            rows = slice(s * GMLP_CHUNK, (s + 1) * GMLP_CHUNK)
            m = mixed[:, s * GMLP_CHUNK:(s + 1) * GMLP_CHUNK] + bs_ref[gi]
            mix_ref[rows, RET_WIDTH + gi * HEAD_DIM:RET_WIDTH + (gi + 1) * HEAD_DIM] = (
                _gelu(u_cols[rows, :]) * m).astype(BF16)

    q, k, v = proj(0), proj(1), proj(2)
    for norm_rows in finish():
        norm_rows(scaled_x)

    s0 = head_scores(0)
    g_halves = [proj_half(3, 0)]
    r0 = head_retention(0, *s0)
    g_halves.append(proj_half(3, 1))
    head_gate(0, r0, g_halves)

    s1 = head_scores(1)
    z_halves = [proj_half(5, 0)]
    r1 = head_retention(1, *s1)
    z_halves.append(proj_half(5, 1))
    head_gate(1, r1, g_halves)

    s2 = head_scores(2)
    u_halves = [proj_half(4, 0)]
    r2 = head_retention(2, *s2)
    u_halves.append(proj_half(4, 1))
    head_gate(2, r2, g_halves)

    s3 = head_scores(3)
    mixed = [group_mix(0, z_halves), group_mix(1, z_halves)]
    r3 = head_retention(3, *s3)
    mixed += [group_mix(2, z_halves), group_mix(3, z_halves)]
    head_gate(3, r3, g_halves)
    for gi in range(GMLP_GROUPS):
        group_gate(gi, mixed[gi], u_halves)

    if first_step:
        ax_ref[...] = scaled_x(slice(None))


def _retention_tables(block):
    log_gamma = np.log1p(-(2.0 ** (-5.0 - np.arange(RET_HEADS, dtype=np.float64))))
    idx = np.arange(block, dtype=np.float64)
    rel = idx[:, None] - idx[None, :]
    decay = np.where(rel[None] >= 0, np.exp(np.maximum(rel, 0.0)[None] * log_gamma[:, None, None]), 0.0)
    xi = np.exp((idx + 1.0)[None, :] * log_gamma[:, None])
    zeta = np.exp((block - 1.0 - idx)[None, :] * log_gamma[:, None])
    widen = lambda t: np.broadcast_to(t[:, :, None], (RET_HEADS, block, HEAD_DIM))
    return tuple(jnp.asarray(t, F32) for t in (decay, widen(xi), widen(zeta)))


def _rope_tables(seq):
    inv_freq = ROPE_BASE ** (-np.arange(0, HEAD_DIM, 2, dtype=np.float64) / HEAD_DIM)
    ang = np.arange(seq, dtype=np.float64)[:, None] * inv_freq[None, :]
    cos, sin = np.cos(ang), np.sin(ang)
    table = np.concatenate([cos, cos, -sin, sin], axis=-1)
    return jnp.asarray(table, F32), jnp.asarray(table * (HEAD_DIM ** -0.5), F32)


def _mixer_call(x, batch, seq, w_in, w_out, gn_g, gn_b, zln_g, zln_b, w_s, b_s, ln_g, ln_b):
    n_seq = seq // MIX_TOKENS
    n_tiles = batch * n_seq
    rope_q, rope_k = _rope_tables(seq)
    decay, xi, zeta = _retention_tables(RET_BLOCK)
    gamma_block = tuple(float((1.0 - 2.0 ** (-5.0 - h)) ** RET_BLOCK) for h in range(RET_HEADS))
    bs_full = jnp.broadcast_to(b_s[:, :, None], (GMLP_GROUPS, GMLP_CHUNK, HEAD_DIM))

    def resident(shape):
        zeros = (0,) * len(shape)
        return pl.BlockSpec(shape, lambda i: zeros, pipeline_mode=pl.Buffered(1))

    tok_in = pl.BlockSpec((MIX_TOKENS, D_MODEL), lambda i: (jnp.minimum(i, n_tiles - 1), 0))
    tok_out = pl.BlockSpec((MIX_TOKENS, D_MODEL), lambda i: (jnp.maximum(i - 1, 0), 0))
    pos = pl.BlockSpec((MIX_TOKENS, 2 * HEAD_DIM), lambda i: (jnp.minimum(i, n_tiles - 1) % n_seq, 0))
    return pl.pallas_call(
        functools.partial(_mixer_kernel, gamma_block=gamma_block, n_seq=n_seq),
        out_shape=jax.ShapeDtypeStruct(x.shape, F32),
        grid=(n_tiles + 1,),
        in_specs=[
            tok_in,
            resident((D_MODEL, IN_WIDTH)),
            resident((D_MODEL, D_MODEL)),
            pos, pos,
            resident((RET_HEADS, RET_BLOCK, RET_BLOCK)),
            resident((RET_HEADS, RET_BLOCK, HEAD_DIM)),
            resident((RET_HEADS, RET_BLOCK, HEAD_DIM)),
            resident((1, RET_WIDTH)), resident((1, RET_WIDTH)),
            resident((1, GMLP_WIDTH)), resident((1, GMLP_WIDTH)),
            resident((GMLP_GROUPS, GMLP_CHUNK, GMLP_CHUNK)),
            resident((GMLP_GROUPS, GMLP_CHUNK, HEAD_DIM)),
            resident((1, D_MODEL)), resident((1, D_MODEL)),
        ],
        out_specs=tok_out,
        scratch_shapes=[
            pltpu.VMEM((RET_HEADS, HEAD_DIM, HEAD_DIM), F32),
            pltpu.VMEM((MIX_TOKENS, D_MODEL), BF16),
            pltpu.VMEM((MIX_TOKENS, D_MODEL), F32),
        ],
        compiler_params=pltpu.CompilerParams(
            dimension_semantics=("arbitrary",), vmem_limit_bytes=VMEM_LIMIT_BYTES),
        name="mixer_ln",
    )(x, w_in, w_out, rope_q, rope_k, decay, xi, zeta, gn_g, gn_b, zln_g, zln_b, w_s, bs_full, ln_g, ln_b)


def kernel(x, ffn1_w_in, ffn1_w_out, ln1_g, ln1_b, mix_w_in, ret_gn_g, ret_gn_b, gmlp_ln_g, gmlp_ln_b,
           gmlp_w_s, gmlp_b_s, mix_w_out, ln2_g, ln2_b, ffn2_w_in, ffn2_w_out, ln3_g, ln3_b):
    batch, seq, d = x.shape
    assert d == D_MODEL and seq % MIX_TOKENS == 0 and (batch * seq) % FFN_TOKENS == 0
    assert MIX_TOKENS % RET_BLOCK == 0 and RET_BLOCK % CHUNK == 0 and MIX_TOKENS % GMLP_CHUNK == 0
    row = lambda t: t.reshape(1, -1)
    h = x.reshape(batch * seq, d)
    for l in range(DEPTH):
        h = _ffn_call(h, ffn1_w_in[l].astype(BF16), ffn1_w_out[l].astype(BF16), row(ln1_g[l]), row(ln1_b[l]))
        h = _mixer_call(h, batch, seq, mix_w_in[l].astype(BF16), mix_w_out[l].astype(BF16),
                        row(ret_gn_g[l]), row(ret_gn_b[l]), row(gmlp_ln_g[l]), row(gmlp_ln_b[l]),
                        gmlp_w_s[l], gmlp_b_s[l], row(ln2_g[l]), row(ln2_b[l]))
        h = _ffn_call(h, ffn2_w_in[l].astype(BF16), ffn2_w_out[l].astype(BF16), row(ln3_g[l]), row(ln3_b[l]))
    return h.reshape(batch, seq, d)
```

```python
import functools

import jax
import jax.numpy as jnp
import numpy as np
from jax import lax
from jax.experimental import pallas as pl
from jax.experimental.pallas import tpu as pltpu

F32 = jnp.float32
BF16 = jnp.bfloat16

D_MODEL = 1024
DEPTH = 1
CHUNK = 64
RET_HEADS = 4
HEAD_DIM = 128
RET_WIDTH = RET_HEADS * HEAD_DIM
GMLP_GROUPS = 4
GMLP_WIDTH = GMLP_GROUPS * HEAD_DIM
GMLP_CHUNK = 128
IN_WIDTH = 4 * RET_WIDTH + 2 * GMLP_WIDTH
D_FF = 2816
ROPE_BASE = 10000.0
LN_EPS = 1e-5
DEEPNORM_ALPHA = (2.0 * DEPTH) ** 0.25

LANES = 128
SUBLANES = 8
MXU_DIM = 256
VMEM_LIMIT_BYTES = 56 * 1024 * 1024

FFN_TOKENS = 1024
FFN_CHUNK = MXU_DIM
MIX_TOKENS = 512
MIX_OUT_ROWS = 256
RET_BLOCK = 256
NORM_ROWS = SUBLANES
FFN_OUT_SPLITS = (256, 512, 768)


def _layer_norm(r, g, b):
    mu = jnp.mean(r, axis=-1, keepdims=True)
    d = r - mu
    var = jnp.mean(d * d, axis=-1, keepdims=True)
    return d * lax.rsqrt(var + LN_EPS) * g + b


def _dot(a, b):
    return jnp.dot(a, b, preferred_element_type=F32)


def _skewed_steps(start_current, finish_previous):
    i = pl.program_id(0)
    last = pl.num_programs(0) - 1

    @pl.when(i == 0)
    def _():
        start_current(True, lambda: [])

    @pl.when(jnp.logical_and(i > 0, i < last))
    def _():
        start_current(False, finish_previous)

    @pl.when(i == last)
    def _():
        for norm_rows in finish_previous():
            norm_rows()


def _row_group_norms(resid_ref, y, first_row, g_ref, b_ref, o_ref):
    def norm_rows(lo, next_resid=None):
        rows = slice(first_row + lo, first_row + lo + NORM_ROWS)
        o_ref[rows, :] = _layer_norm(resid_ref[rows, :] + y[lo:lo + NORM_ROWS, :], g_ref[...], b_ref[...])
        if next_resid is not None:
            resid_ref[rows, :] = next_resid(rows)
    return [functools.partial(norm_rows, lo) for lo in range(0, y.shape[0], NORM_ROWS)]


def _ffn_kernel(x_ref, win_ref, wout_ref, g_ref, b_ref, o_ref, act_ref):
    xb = x_ref[...].astype(BF16)
    for j in range(D_FF // FFN_CHUNK):
        lo = j * FFN_CHUNK
        gate = _dot(xb, win_ref[:, lo:lo + FFN_CHUNK])
        up = _dot(xb, win_ref[:, D_FF + lo:D_FF + lo + FFN_CHUNK])
        act_ref[:, lo:lo + FFN_CHUNK] = (gate * jax.nn.sigmoid(gate) * up).astype(BF16)
    for lo, hi in zip((0,) + FFN_OUT_SPLITS, FFN_OUT_SPLITS + (FFN_TOKENS,)):
        rows = slice(lo, hi)
        y = _dot(act_ref[rows, :], wout_ref[...])
        o_ref[rows, :] = _layer_norm(DEEPNORM_ALPHA * x_ref[rows, :] + 0.5 * y, g_ref[...], b_ref[...])


def _ffn_call(x, w_in, w_out, ln_g, ln_b):
    n = x.shape[0]
    const = lambda i: (0, 0)
    resident = functools.partial(pl.BlockSpec, index_map=const, pipeline_mode=pl.Buffered(1))
    return pl.pallas_call(
        _ffn_kernel,
        out_shape=jax.ShapeDtypeStruct((n, D_MODEL), F32),
        grid=(n // FFN_TOKENS,),
        in_specs=[
            pl.BlockSpec((FFN_TOKENS, D_MODEL), lambda i: (i, 0)),
            resident((D_MODEL, 2 * D_FF)),
            resident((D_FF, D_MODEL)),
            resident((1, D_MODEL)),
            resident((1, D_MODEL)),
        ],
        out_specs=pl.BlockSpec((FFN_TOKENS, D_MODEL), lambda i: (i, 0)),
        scratch_shapes=[pltpu.VMEM((FFN_TOKENS, D_FF), BF16)],
        compiler_params=pltpu.CompilerParams(
            dimension_semantics=("arbitrary",), vmem_limit_bytes=VMEM_LIMIT_BYTES),
        name="ffn_ln",
    )(x, w_in, w_out, ln_g, ln_b)


def _rope(t, cos, sin_signed):
    return t * cos + pltpu.roll(t, HEAD_DIM // 2, 1) * sin_signed


def _gelu(t):
    return 0.5 * t * (1.0 + lax.erf(t * (0.5 ** 0.5)))


def _mixer_kernel(x_ref, win_ref, wout_ref, ropeq_ref, ropek_ref, decay_ref, xi_ref, zeta_ref,
                  gn_g_ref, gn_b_ref, zln_g_ref, zln_b_ref, ws_ref, bs_ref, ln_g_ref, ln_b_ref,
                  o_ref, state_ref, mix_ref, ax_ref, *, gamma_block, n_seq):
    i = pl.program_id(0)

    def start_current(first_step, finish):
        _mix_tile(x_ref, win_ref, ropeq_ref, ropek_ref, decay_ref, xi_ref, zeta_ref,
                  gn_g_ref, gn_b_ref, zln_g_ref, zln_b_ref, ws_ref, bs_ref, state_ref,
                  mix_ref, ax_ref, finish,
                  first_step=first_step, first_of_sequence=i % n_seq == 0, gamma_block=gamma_block)

    def finish_previous():
        norms = []
        for lo in range(0, MIX_TOKENS, MIX_OUT_ROWS):
            y = _dot(mix_ref[lo:lo + MIX_OUT_ROWS, :], wout_ref[...])
            norms += _row_group_norms(ax_ref, y, lo, ln_g_ref, ln_b_ref, o_ref)
        return norms

    _skewed_steps(start_current, finish_previous)


def _mix_tile(x_ref, win_ref, ropeq_ref, ropek_ref, decay_ref, xi_ref, zeta_ref,
              gn_g_ref, gn_b_ref, zln_g_ref, zln_b_ref, ws_ref, bs_ref, state_ref, mix_ref, ax_ref, finish,
              *, first_step, first_of_sequence, gamma_block):
    xb = x_ref[...].astype(BF16)
    blocks = range(MIX_TOKENS // RET_BLOCK)
    block_rows = [slice(c * RET_BLOCK, (c + 1) * RET_BLOCK) for c in blocks]
    half = RET_WIDTH // 2
    contract_last = (((1,), (1,)), ((), ()))
    contract_first = (((0,), (0,)), ((), ()))

    def proj(k):
        return _dot(xb, win_ref[:, k * RET_WIDTH:(k + 1) * RET_WIDTH])

    def proj_half(k, part):
        lo = k * RET_WIDTH + part * half
        return _dot(xb, win_ref[:, lo:lo + half])

    def unit_cols(halves, i):
        return halves[i // 2][:, (i % 2) * HEAD_DIM:(i % 2 + 1) * HEAD_DIM]

    def scaled_x(rows):
        return DEEPNORM_ALPHA * x_ref[rows, :]

    def head_scores(h):
        cols = slice(h * HEAD_DIM, (h + 1) * HEAD_DIM)
        qr, scores, kv = [], [], []
        for rows in block_rows:
            qc = _rope(q[rows, cols], ropeq_ref[rows, :HEAD_DIM], ropeq_ref[rows, HEAD_DIM:]).astype(BF16)
            kc = _rope(k[rows, cols], ropek_ref[rows, :HEAD_DIM], ropek_ref[rows, HEAD_DIM:]).astype(BF16)
            qr.append(qc)
            scores.append(lax.dot_general(qc, kc, contract_last, preferred_element_type=F32))
            kv.append(lax.dot_general(kc, (v[rows, cols] * zeta_ref[h]).astype(BF16), contract_first,
                                      preferred_element_type=F32))
        return qr, scores, kv

    def head_retention(h, qr, scores, kv):
        cols = slice(h * HEAD_DIM, (h + 1) * HEAD_DIM)
        if first_step:
            state = jnp.zeros((HEAD_DIM, HEAD_DIM), F32)
        else:
            state = state_ref[h]
            state = jnp.where(first_of_sequence, jnp.zeros_like(state), state)
        ret = []
        for c in blocks:
            decayed = (scores[c] * decay_ref[h]).astype(BF16)
            intra = _dot(decayed, v[block_rows[c], cols].astype(BF16))
            cross = _dot(qr[c], state.astype(BF16)) * xi_ref[h]
            ret.append(intra + cross)
            state = gamma_block[h] * state + kv[c]
        state_ref[h] = state
        return ret

    def head_gate(h, ret, g_halves):
        cols = slice(h * HEAD_DIM, (h + 1) * HEAD_DIM)
        gate_cols = unit_cols(g_halves, h)
        for c in blocks:
            gate = gate_cols[block_rows[c], :]
            normed = _layer_norm(ret[c], gn_g_ref[:, cols], gn_b_ref[:, cols])
            mix_ref[block_rows[c], cols] = (gate * jax.nn.sigmoid(gate) * normed).astype(BF16)

    n_sub = MIX_TOKENS // GMLP_CHUNK
    row_id = lax.broadcasted_iota(jnp.int32, (GMLP_CHUNK, GMLP_CHUNK), 0)
    col_id = lax.broadcasted_iota(jnp.int32, (GMLP_CHUNK, GMLP_CHUNK), 1)
    causal = row_id >= col_id

    def group_mix(gi, z_halves):
        cols = slice(gi * HEAD_DIM, (gi + 1) * HEAD_DIM)
        zn = _layer_norm(_gelu(unit_cols(z_halves, gi)), zln_g_ref[:, cols], zln_b_ref[:, cols]).astype(BF16)
        zcat = jnp.concatenate([zn[s * GMLP_CHUNK:(s + 1) * GMLP_CHUNK] for s in range(n_sub)], axis=1)
        w = jnp.where(causal, ws_ref[gi], 0.0).astype(BF16)
        return _dot(w, zcat)

    def group_gate(gi, mixed, u_halves):
        u_cols = unit_cols(u_halves, gi)
        for s in range(n_sub):
            rows = slice(s * GMLP_CHUNK, (s + 1) * GMLP_CHUNK)
            m = mixed[:, s * GMLP_CHUNK:(s + 1) * GMLP_CHUNK] + bs_ref[gi]
            mix_ref[rows, RET_WIDTH + gi * HEAD_DIM:RET_WIDTH + (gi + 1) * HEAD_DIM] = (
                _gelu(u_cols[rows, :]) * m).astype(BF16)

    q, k, v = proj(0), proj(1), proj(2)
    for norm_rows in finish():
        norm_rows(scaled_x)

    g_halves = [proj_half(3, 0), proj_half(3, 1)]
    fillers = [(5, 0), (5, 1), (4, 0), (4, 1)]
    slabs = []
    for h in range(RET_HEADS):
        s = head_scores(h)
        slabs.append(proj_half(*fillers[h]))
        r = head_retention(h, *s)
        head_gate(h, r, g_halves)
    z_halves, u_halves = slabs[:2], slabs[2:]
    mixed = [group_mix(gi, z_halves) for gi in range(GMLP_GROUPS)]
    for gi in range(GMLP_GROUPS):
        group_gate(gi, mixed[gi], u_halves)

    if first_step:
        ax_ref[...] = scaled_x(slice(None))


def _retention_tables(block):
    log_gamma = np.log1p(-(2.0 ** (-5.0 - np.arange(RET_HEADS, dtype=np.float64))))
    idx = np.arange(block, dtype=np.float64)
    rel = idx[:, None] - idx[None, :]
    decay = np.where(rel[None] >= 0, np.exp(np.maximum(rel, 0.0)[None] * log_gamma[:, None, None]), 0.0)
    xi = np.exp((idx + 1.0)[None, :] * log_gamma[:, None])
    zeta = np.exp((block - 1.0 - idx)[None, :] * log_gamma[:, None])
    widen = lambda t: np.broadcast_to(t[:, :, None], (RET_HEADS, block, HEAD_DIM))
    return tuple(jnp.asarray(t, F32) for t in (decay, widen(xi), widen(zeta)))


def _rope_tables(seq):
    inv_freq = ROPE_BASE ** (-np.arange(0, HEAD_DIM, 2, dtype=np.float64) / HEAD_DIM)
    ang = np.arange(seq, dtype=np.float64)[:, None] * inv_freq[None, :]
    cos, sin = np.cos(ang), np.sin(ang)
    table = np.concatenate([cos, cos, -sin, sin], axis=-1)
    return jnp.asarray(table, F32), jnp.asarray(table * (HEAD_DIM ** -0.5), F32)


def _mixer_call(x, batch, seq, w_in, w_out, gn_g, gn_b, zln_g, zln_b, w_s, b_s, ln_g, ln_b):
    n_seq = seq // MIX_TOKENS
    n_tiles = batch * n_seq
    rope_q, rope_k = _rope_tables(seq)
    decay, xi, zeta = _retention_tables(RET_BLOCK)
    gamma_block = tuple(float((1.0 - 2.0 ** (-5.0 - h)) ** RET_BLOCK) for h in range(RET_HEADS))
    bs_full = jnp.broadcast_to(b_s[:, :, None], (GMLP_GROUPS, GMLP_CHUNK, HEAD_DIM))

    def resident(shape):
        zeros = (0,) * len(shape)
        return pl.BlockSpec(shape, lambda i: zeros, pipeline_mode=pl.Buffered(1))

    tok_in = pl.BlockSpec((MIX_TOKENS, D_MODEL), lambda i: (jnp.minimum(i, n_tiles - 1), 0))
    tok_out = pl.BlockSpec((MIX_TOKENS, D_MODEL), lambda i: (jnp.maximum(i - 1, 0), 0))
    pos = pl.BlockSpec((MIX_TOKENS, 2 * HEAD_DIM), lambda i: (jnp.minimum(i, n_tiles - 1) % n_seq, 0))
    return pl.pallas_call(
        functools.partial(_mixer_kernel, gamma_block=gamma_block, n_seq=n_seq),
        out_shape=jax.ShapeDtypeStruct(x.shape, F32),
        grid=(n_tiles + 1,),
        in_specs=[
            tok_in,
            resident((D_MODEL, IN_WIDTH)),
            resident((D_MODEL, D_MODEL)),
            pos, pos,
            resident((RET_HEADS, RET_BLOCK, RET_BLOCK)),
            resident((RET_HEADS, RET_BLOCK, HEAD_DIM)),
            resident((RET_HEADS, RET_BLOCK, HEAD_DIM)),
            resident((1, RET_WIDTH)), resident((1, RET_WIDTH)),
            resident((1, GMLP_WIDTH)), resident((1, GMLP_WIDTH)),
            resident((GMLP_GROUPS, GMLP_CHUNK, GMLP_CHUNK)),
            resident((GMLP_GROUPS, GMLP_CHUNK, HEAD_DIM)),
            resident((1, D_MODEL)), resident((1, D_MODEL)),
        ],
        out_specs=tok_out,
        scratch_shapes=[
            pltpu.VMEM((RET_HEADS, HEAD_DIM, HEAD_DIM), F32),
            pltpu.VMEM((MIX_TOKENS, D_MODEL), BF16),
            pltpu.VMEM((MIX_TOKENS, D_MODEL), F32),
        ],
        compiler_params=pltpu.CompilerParams(
            dimension_semantics=("arbitrary",), vmem_limit_bytes=VMEM_LIMIT_BYTES),
        name="mixer_ln",
    )(x, w_in, w_out, rope_q, rope_k, decay, xi, zeta, gn_g, gn_b, zln_g, zln_b, w_s, bs_full, ln_g, ln_b)


def kernel(x, ffn1_w_in, ffn1_w_out, ln1_g, ln1_b, mix_w_in, ret_gn_g, ret_gn_b, gmlp_ln_g, gmlp_ln_b,
           gmlp_w_s, gmlp_b_s, mix_w_out, ln2_g, ln2_b, ffn2_w_in, ffn2_w_out, ln3_g, ln3_b):
    batch, seq, d = x.shape
    assert d == D_MODEL and seq % MIX_TOKENS == 0 and (batch * seq) % FFN_TOKENS == 0
    assert MIX_TOKENS % RET_BLOCK == 0 and RET_BLOCK % CHUNK == 0 and MIX_TOKENS % GMLP_CHUNK == 0
    row = lambda t: t.reshape(1, -1)
    h = x.reshape(batch * seq, d)
    for l in range(DEPTH):
        h = _ffn_call(h, ffn1_w_in[l].astype(BF16), ffn1_w_out[l].astype(BF16), row(ln1_g[l]), row(ln1_b[l]))
        h = _mixer_call(h, batch, seq, mix_w_in[l].astype(BF16), mix_w_out[l].astype(BF16),
                        row(ret_gn_g[l]), row(ret_gn_b[l]), row(gmlp_ln_g[l]), row(gmlp_ln_b[l]),
                        gmlp_w_s[l], gmlp_b_s[l], row(ln2_g[l]), row(ln2_b[l]))
        h = _ffn_call(h, ffn2_w_in[l].astype(BF16), ffn2_w_out[l].astype(BF16), row(ln3_g[l]), row(ln3_b[l]))
    return h.reshape(batch, seq, d)
```

```python
import functools

import jax
import jax.numpy as jnp
import numpy as np
from jax import lax
from jax.experimental import pallas as pl
from jax.experimental.pallas import tpu as pltpu

F32 = jnp.float32
BF16 = jnp.bfloat16

D_MODEL = 1024
DEPTH = 1
CHUNK = 64
RET_HEADS = 4
HEAD_DIM = 128
RET_WIDTH = RET_HEADS * HEAD_DIM
GMLP_GROUPS = 4
GMLP_WIDTH = GMLP_GROUPS * HEAD_DIM
GMLP_CHUNK = 128
IN_WIDTH = 4 * RET_WIDTH + 2 * GMLP_WIDTH
D_FF = 2816
ROPE_BASE = 10000.0
LN_EPS = 1e-5
DEEPNORM_ALPHA = (2.0 * DEPTH) ** 0.25

LANES = 128
SUBLANES = 8
MXU_DIM = 256
VMEM_LIMIT_BYTES = 56 * 1024 * 1024

FFN_TOKENS = 1024
FFN_CHUNK = MXU_DIM
MIX_TOKENS = 512
MIX_OUT_ROWS = 256
RET_BLOCK = 256
NORM_ROWS = SUBLANES
FFN_OUT_SPLITS = (256, 512, 768)


def _layer_norm(r, g, b):
    mu = jnp.mean(r, axis=-1, keepdims=True)
    d = r - mu
    var = jnp.mean(d * d, axis=-1, keepdims=True)
    return d * lax.rsqrt(var + LN_EPS) * g + b


def _dot(a, b):
    return jnp.dot(a, b, preferred_element_type=F32)


def _skewed_steps(start_current, finish_previous):
    i = pl.program_id(0)
    last = pl.num_programs(0) - 1

    @pl.when(i == 0)
    def _():
        start_current(True, lambda: [])

    @pl.when(jnp.logical_and(i > 0, i < last))
    def _():
        start_current(False, finish_previous)

    @pl.when(i == last)
    def _():
        for norm_rows in finish_previous():
            norm_rows()


def _row_group_norms(resid, y, first_row, g_ref, b_ref, o_ref):
    def norm_rows(lo):
        rows = slice(first_row + lo, first_row + lo + NORM_ROWS)
        o_ref[rows, :] = _layer_norm(resid(rows) + y[lo:lo + NORM_ROWS, :], g_ref[...], b_ref[...])
    return [functools.partial(norm_rows, lo) for lo in range(0, y.shape[0], NORM_ROWS)]


def _ffn_kernel(x_ref, win_ref, wout_ref, g_ref, b_ref, o_ref, act_ref):
    xb = x_ref[...].astype(BF16)
    for j in range(D_FF // FFN_CHUNK):
        lo = j * FFN_CHUNK
        gate = _dot(xb, win_ref[:, lo:lo + FFN_CHUNK])
        up = _dot(xb, win_ref[:, D_FF + lo:D_FF + lo + FFN_CHUNK])
        act_ref[:, lo:lo + FFN_CHUNK] = (gate * jax.nn.sigmoid(gate) * up).astype(BF16)
    for lo, hi in zip((0,) + FFN_OUT_SPLITS, FFN_OUT_SPLITS + (FFN_TOKENS,)):
        rows = slice(lo, hi)
        y = _dot(act_ref[rows, :], wout_ref[...])
        o_ref[rows, :] = _layer_norm(DEEPNORM_ALPHA * x_ref[rows, :] + 0.5 * y, g_ref[...], b_ref[...])


def _ffn_call(x, w_in, w_out, ln_g, ln_b):
    n = x.shape[0]
    const = lambda i: (0, 0)
    resident = functools.partial(pl.BlockSpec, index_map=const, pipeline_mode=pl.Buffered(1))
    return pl.pallas_call(
        _ffn_kernel,
        out_shape=jax.ShapeDtypeStruct((n, D_MODEL), F32),
        grid=(n // FFN_TOKENS,),
        in_specs=[
            pl.BlockSpec((FFN_TOKENS, D_MODEL), lambda i: (i, 0)),
            resident((D_MODEL, 2 * D_FF)),
            resident((D_FF, D_MODEL)),
            resident((1, D_MODEL)),
            resident((1, D_MODEL)),
        ],
        out_specs=pl.BlockSpec((FFN_TOKENS, D_MODEL), lambda i: (i, 0)),
        scratch_shapes=[pltpu.VMEM((FFN_TOKENS, D_FF), BF16)],
        compiler_params=pltpu.CompilerParams(
            dimension_semantics=("arbitrary",), vmem_limit_bytes=VMEM_LIMIT_BYTES),
        name="ffn_ln",
    )(x, w_in, w_out, ln_g, ln_b)


def _rope(t, cos, sin_signed):
    return t * cos + pltpu.roll(t, HEAD_DIM // 2, 1) * sin_signed


def _gelu(t):
    return 0.5 * t * (1.0 + lax.erf(t * (0.5 ** 0.5)))


def _mixer_kernel(x_ref, xprev_ref, win_ref, wout_ref, ropeq_ref, ropek_ref, decay_ref, xi_ref, zeta_ref,
                  gn_g_ref, gn_b_ref, zln_g_ref, zln_b_ref, ws_ref, bs_ref, ln_g_ref, ln_b_ref,
                  o_ref, state_ref, mix_ref, *, gamma_block, n_seq):
    i = pl.program_id(0)

    def start_current(first_step, finish):
        _mix_tile(x_ref, win_ref, ropeq_ref, ropek_ref, decay_ref, xi_ref, zeta_ref,
                  gn_g_ref, gn_b_ref, zln_g_ref, zln_b_ref, ws_ref, bs_ref, state_ref,
                  mix_ref, finish,
                  first_step=first_step, first_of_sequence=i % n_seq == 0, gamma_block=gamma_block)

    def finish_previous():
        norms = []
        for lo in range(0, MIX_TOKENS, MIX_OUT_ROWS):
            y = _dot(mix_ref[lo:lo + MIX_OUT_ROWS, :], wout_ref[...])
            norms += _row_group_norms(lambda rows: DEEPNORM_ALPHA * xprev_ref[rows, :], y, lo,
                                      ln_g_ref, ln_b_ref, o_ref)
        return norms

    _skewed_steps(start_current, finish_previous)


def _mix_tile(x_ref, win_ref, ropeq_ref, ropek_ref, decay_ref, xi_ref, zeta_ref,
              gn_g_ref, gn_b_ref, zln_g_ref, zln_b_ref, ws_ref, bs_ref, state_ref, mix_ref, finish,
              *, first_step, first_of_sequence, gamma_block):
    xb = x_ref[...].astype(BF16)
    blocks = range(MIX_TOKENS // RET_BLOCK)
    block_rows = [slice(c * RET_BLOCK, (c + 1) * RET_BLOCK) for c in blocks]
    half = RET_WIDTH // 2
    contract_last = (((1,), (1,)), ((), ()))
    contract_first = (((0,), (0,)), ((), ()))

    def proj(k):
        return _dot(xb, win_ref[:, k * RET_WIDTH:(k + 1) * RET_WIDTH])

    def proj_half(k, part):
        lo = k * RET_WIDTH + part * half
        return _dot(xb, win_ref[:, lo:lo + half])

    def unit_cols(halves, i):
        return halves[i // 2][:, (i % 2) * HEAD_DIM:(i % 2 + 1) * HEAD_DIM]

    def head_scores(h):
        cols = slice(h * HEAD_DIM, (h + 1) * HEAD_DIM)
        qr, scores, kv = [], [], []
        for rows in block_rows:
            qc = _rope(q[rows, cols], ropeq_ref[rows, :HEAD_DIM], ropeq_ref[rows, HEAD_DIM:]).astype(BF16)
            kc = _rope(k[rows, cols], ropek_ref[rows, :HEAD_DIM], ropek_ref[rows, HEAD_DIM:]).astype(BF16)
            qr.append(qc)
            scores.append(lax.dot_general(qc, kc, contract_last, preferred_element_type=F32))
            kv.append(lax.dot_general(kc, (v[rows, cols] * zeta_ref[h]).astype(BF16), contract_first,
                                      preferred_element_type=F32))
        return qr, scores, kv

    def head_retention(h, qr, scores, kv):
        cols = slice(h * HEAD_DIM, (h + 1) * HEAD_DIM)
        if first_step:
            state = jnp.zeros((HEAD_DIM, HEAD_DIM), F32)
        else:
            state = state_ref[h]
            state = jnp.where(first_of_sequence, jnp.zeros_like(state), state)
        ret = []
        for c in blocks:
            decayed = (scores[c] * decay_ref[h]).astype(BF16)
            intra = _dot(decayed, v[block_rows[c], cols].astype(BF16))
            cross = _dot(qr[c], state.astype(BF16)) * xi_ref[h]
            ret.append(intra + cross)
            state = gamma_block[h] * state + kv[c]
        state_ref[h] = state
        return ret

    def head_gate(h, ret, g_halves):
        cols = slice(h * HEAD_DIM, (h + 1) * HEAD_DIM)
        gate_cols = unit_cols(g_halves, h)
        for c in blocks:
            gate = gate_cols[block_rows[c], :]
            normed = _layer_norm(ret[c], gn_g_ref[:, cols], gn_b_ref[:, cols])
            mix_ref[block_rows[c], cols] = (gate * jax.nn.sigmoid(gate) * normed).astype(BF16)

    n_sub = MIX_TOKENS // GMLP_CHUNK
    row_id = lax.broadcasted_iota(jnp.int32, (GMLP_CHUNK, GMLP_CHUNK), 0)
    col_id = lax.broadcasted_iota(jnp.int32, (GMLP_CHUNK, GMLP_CHUNK), 1)
    causal = row_id >= col_id

    def group_mix(gi, z_halves):
        cols = slice(gi * HEAD_DIM, (gi + 1) * HEAD_DIM)
        zn = _layer_norm(_gelu(unit_cols(z_halves, gi)), zln_g_ref[:, cols], zln_b_ref[:, cols]).astype(BF16)
        zcat = jnp.concatenate([zn[s * GMLP_CHUNK:(s + 1) * GMLP_CHUNK] for s in range(n_sub)], axis=1)
        w = jnp.where(causal, ws_ref[gi], 0.0).astype(BF16)
        return _dot(w, zcat)

    def group_gate(gi, mixed, u_halves):
        u_cols = unit_cols(u_halves, gi)
        for s in range(n_sub):
            rows = slice(s * GMLP_CHUNK, (s + 1) * GMLP_CHUNK)
            m = mixed[:, s * GMLP_CHUNK:(s + 1) * GMLP_CHUNK] + bs_ref[gi]
            mix_ref[rows, RET_WIDTH + gi * HEAD_DIM:RET_WIDTH + (gi + 1) * HEAD_DIM] = (
                _gelu(u_cols[rows, :]) * m).astype(BF16)

    q, k, v = proj(0), proj(1), proj(2)
    for norm_rows in finish():
        norm_rows()

    g_halves = [proj_half(3, 0), proj_half(3, 1)]
    fillers = [(5, 0), (5, 1), (4, 0), (4, 1)]
    slabs = []
    for h in range(RET_HEADS):
        s = head_scores(h)
        slabs.append(proj_half(*fillers[h]))
        r = head_retention(h, *s)
        head_gate(h, r, g_halves)
    z_halves, u_halves = slabs[:2], slabs[2:]
    mixed = [group_mix(gi, z_halves) for gi in range(GMLP_GROUPS)]
    for gi in range(GMLP_GROUPS):
        group_gate(gi, mixed[gi], u_halves)


def _retention_tables(block):
    log_gamma = np.log1p(-(2.0 ** (-5.0 - np.arange(RET_HEADS, dtype=np.float64))))
    idx = np.arange(block, dtype=np.float64)
    rel = idx[:, None] - idx[None, :]
    decay = np.where(rel[None] >= 0, np.exp(np.maximum(rel, 0.0)[None] * log_gamma[:, None, None]), 0.0)
    xi = np.exp((idx + 1.0)[None, :] * log_gamma[:, None])
    zeta = np.exp((block - 1.0 - idx)[None, :] * log_gamma[:, None])
    widen = lambda t: np.broadcast_to(t[:, :, None], (RET_HEADS, block, HEAD_DIM))
    return tuple(jnp.asarray(t, F32) for t in (decay, widen(xi), widen(zeta)))


def _rope_tables(seq):
    inv_freq = ROPE_BASE ** (-np.arange(0, HEAD_DIM, 2, dtype=np.float64) / HEAD_DIM)
    ang = np.arange(seq, dtype=np.float64)[:, None] * inv_freq[None, :]
    cos, sin = np.cos(ang), np.sin(ang)
    table = np.concatenate([cos, cos, -sin, sin], axis=-1)
    return jnp.asarray(table, F32), jnp.asarray(table * (HEAD_DIM ** -0.5), F32)


def _mixer_call(x, batch, seq, w_in, w_out, gn_g, gn_b, zln_g, zln_b, w_s, b_s, ln_g, ln_b):
    n_seq = seq // MIX_TOKENS
    n_tiles = batch * n_seq
    rope_q, rope_k = _rope_tables(seq)
    decay, xi, zeta = _retention_tables(RET_BLOCK)
    gamma_block = tuple(float((1.0 - 2.0 ** (-5.0 - h)) ** RET_BLOCK) for h in range(RET_HEADS))
    bs_full = jnp.broadcast_to(b_s[:, :, None], (GMLP_GROUPS, GMLP_CHUNK, HEAD_DIM))

    def resident(shape):
        zeros = (0,) * len(shape)
        return pl.BlockSpec(shape, lambda i: zeros, pipeline_mode=pl.Buffered(1))

    tok_in = pl.BlockSpec((MIX_TOKENS, D_MODEL), lambda i: (jnp.minimum(i, n_tiles - 1), 0))
    previous = lambda i: (jnp.maximum(i - 1, 0), 0)
    tok_prev = pl.BlockSpec((MIX_TOKENS, D_MODEL), previous)
    tok_out = pl.BlockSpec((MIX_TOKENS, D_MODEL), previous)
    pos = pl.BlockSpec((MIX_TOKENS, 2 * HEAD_DIM), lambda i: (jnp.minimum(i, n_tiles - 1) % n_seq, 0))
    return pl.pallas_call(
        functools.partial(_mixer_kernel, gamma_block=gamma_block, n_seq=n_seq),
        out_shape=jax.ShapeDtypeStruct(x.shape, F32),
        grid=(n_tiles + 1,),
        in_specs=[
            tok_in, tok_prev,
            resident((D_MODEL, IN_WIDTH)),
            resident((D_MODEL, D_MODEL)),
            pos, pos,
            resident((RET_HEADS, RET_BLOCK, RET_BLOCK)),
            resident((RET_HEADS, RET_BLOCK, HEAD_DIM)),
            resident((RET_HEADS, RET_BLOCK, HEAD_DIM)),
            resident((1, RET_WIDTH)), resident((1, RET_WIDTH)),
            resident((1, GMLP_WIDTH)), resident((1, GMLP_WIDTH)),
            resident((GMLP_GROUPS, GMLP_CHUNK, GMLP_CHUNK)),
            resident((GMLP_GROUPS, GMLP_CHUNK, HEAD_DIM)),
            resident((1, D_MODEL)), resident((1, D_MODEL)),
        ],
        out_specs=tok_out,
        scratch_shapes=[
            pltpu.VMEM((RET_HEADS, HEAD_DIM, HEAD_DIM), F32),
            pltpu.VMEM((MIX_TOKENS, D_MODEL), BF16),
        ],
        compiler_params=pltpu.CompilerParams(
            dimension_semantics=("arbitrary",), vmem_limit_bytes=VMEM_LIMIT_BYTES),
        name="mixer_ln",
    )(x, x, w_in, w_out, rope_q, rope_k, decay, xi, zeta, gn_g, gn_b, zln_g, zln_b, w_s, bs_full, ln_g, ln_b)


def kernel(x, ffn1_w_in, ffn1_w_out, ln1_g, ln1_b, mix_w_in, ret_gn_g, ret_gn_b, gmlp_ln_g, gmlp_ln_b,
           gmlp_w_s, gmlp_b_s, mix_w_out, ln2_g, ln2_b, ffn2_w_in, ffn2_w_out, ln3_g, ln3_b):
    batch, seq, d = x.shape
    assert d == D_MODEL and seq % MIX_TOKENS == 0 and (batch * seq) % FFN_TOKENS == 0
    assert MIX_TOKENS % RET_BLOCK == 0 and RET_BLOCK % CHUNK == 0 and MIX_TOKENS % GMLP_CHUNK == 0
    row = lambda t: t.reshape(1, -1)
    h = x.reshape(batch * seq, d)
    for l in range(DEPTH):
        h = _ffn_call(h, ffn1_w_in[l].astype(BF16), ffn1_w_out[l].astype(BF16), row(ln1_g[l]), row(ln1_b[l]))
        h = _mixer_call(h, batch, seq, mix_w_in[l].astype(BF16), mix_w_out[l].astype(BF16),
                        row(ret_gn_g[l]), row(ret_gn_b[l]), row(gmlp_ln_g[l]), row(gmlp_ln_b[l]),
                        gmlp_w_s[l], gmlp_b_s[l], row(ln2_g[l]), row(ln2_b[l]))
        h = _ffn_call(h, ffn2_w_in[l].astype(BF16), ffn2_w_out[l].astype(BF16), row(ln3_g[l]), row(ln3_b[l]))
    return h.reshape(batch, seq, d)
```

```python
import functools

import jax
import jax.numpy as jnp
import numpy as np
from jax import lax
from jax.experimental import pallas as pl
from jax.experimental.pallas import tpu as pltpu

F32 = jnp.float32
BF16 = jnp.bfloat16

D_MODEL = 1024
DEPTH = 1
CHUNK = 64
RET_HEADS = 4
HEAD_DIM = 128
RET_WIDTH = RET_HEADS * HEAD_DIM
GMLP_GROUPS = 4
GMLP_WIDTH = GMLP_GROUPS * HEAD_DIM
GMLP_CHUNK = 128
IN_WIDTH = 4 * RET_WIDTH + 2 * GMLP_WIDTH
D_FF = 2816
ROPE_BASE = 10000.0
LN_EPS = 1e-5
DEEPNORM_ALPHA = (2.0 * DEPTH) ** 0.25

LANES = 128
SUBLANES = 8
BF16_ROWS = 16
MXU_DIM = 256
VMEM_LIMIT_BYTES = 56 * 1024 * 1024

FFN_TOKENS = 1024
FFN_CHUNK = MXU_DIM
MIX_TOKENS = 512
MIX_OUT_ROWS = 256
RET_BLOCK = 256
NORM_ROWS = SUBLANES
FFN_OUT_SPLITS = (256, 512, 768)


def _layer_norm(r, g, b):
    mu = jnp.mean(r, axis=-1, keepdims=True)
    d = r - mu
    var = jnp.mean(d * d, axis=-1, keepdims=True)
    return d * lax.rsqrt(var + LN_EPS) * g + b


def _dot(a, b):
    return jnp.dot(a, b, preferred_element_type=F32)


def _skewed_steps(start_current, finish_previous):
    i = pl.program_id(0)
    last = pl.num_programs(0) - 1

    @pl.when(i == 0)
    def _():
        start_current(True, lambda: [])

    @pl.when(jnp.logical_and(i > 0, i < last))
    def _():
        start_current(False, finish_previous)

    @pl.when(i == last)
    def _():
        for norm_rows in finish_previous():
            norm_rows()


def _row_group_norms(resid, y, first_row, g_ref, b_ref, o_ref):
    def norm_rows(lo):
        rows = slice(first_row + lo, first_row + lo + NORM_ROWS)
        o_ref[rows, :] = _layer_norm(resid(rows) + y[lo:lo + NORM_ROWS, :], g_ref[...], b_ref[...])
    return [functools.partial(norm_rows, lo) for lo in range(0, y.shape[0], NORM_ROWS)]


def _ffn_kernel(n_cast, x_ref, win_ref, wout_ref, g_ref, b_ref, *refs):
    cast_src, o_ref, cast_dst, act_ref = refs[:n_cast], refs[n_cast], refs[n_cast + 1:-1], refs[-1]
    xb = x_ref[...].astype(BF16)
    for j in range(D_FF // FFN_CHUNK):
        lo = j * FFN_CHUNK
        gate = _dot(xb, win_ref[:, lo:lo + FFN_CHUNK])
        up = _dot(xb, win_ref[:, D_FF + lo:D_FF + lo + FFN_CHUNK])
        act_ref[:, lo:lo + FFN_CHUNK] = (gate * jax.nn.sigmoid(gate) * up).astype(BF16)
    for src_ref, dst_ref in zip(cast_src, cast_dst):
        dst_ref[...] = src_ref[...].astype(BF16)
    for lo, hi in zip((0,) + FFN_OUT_SPLITS, FFN_OUT_SPLITS + (FFN_TOKENS,)):
        rows = slice(lo, hi)
        y = _dot(act_ref[rows, :], wout_ref[...])
        o_ref[rows, :] = _layer_norm(DEEPNORM_ALPHA * x_ref[rows, :] + 0.5 * y, g_ref[...], b_ref[...])


def _row_chunk_spec(shape, n_steps):
    rows = next(r for r in range(BF16_ROWS, shape[0] + 1, BF16_ROWS)
                if shape[0] % r == 0 and shape[0] // r <= n_steps)
    last = shape[0] // rows - 1
    return pl.BlockSpec((rows, shape[1]), lambda i: (jnp.minimum(i, last), 0))


def _ffn_call(x, w_in, w_out, ln_g, ln_b, cast_weights=()):
    n = x.shape[0]
    n_steps = n // FFN_TOKENS
    const = lambda i: (0, 0)
    resident = functools.partial(pl.BlockSpec, index_map=const, pipeline_mode=pl.Buffered(1))
    tile = lambda: pl.BlockSpec((FFN_TOKENS, D_MODEL), lambda i: (i, 0))
    chunks = lambda: [_row_chunk_spec(w.shape, n_steps) for w in cast_weights]
    return pl.pallas_call(
        functools.partial(_ffn_kernel, len(cast_weights)),
        out_shape=[jax.ShapeDtypeStruct((n, D_MODEL), F32)] +
                  [jax.ShapeDtypeStruct(w.shape, BF16) for w in cast_weights],
        grid=(n_steps,),
        in_specs=[
            tile(),
            resident((D_MODEL, 2 * D_FF)),
            resident((D_FF, D_MODEL)),
            resident((1, D_MODEL)),
            resident((1, D_MODEL)),
        ] + chunks(),
        out_specs=[tile()] + chunks(),
        scratch_shapes=[pltpu.VMEM((FFN_TOKENS, D_FF), BF16)],
        compiler_params=pltpu.CompilerParams(
            dimension_semantics=("arbitrary",), vmem_limit_bytes=VMEM_LIMIT_BYTES),
        name="ffn_ln",
    )(x, w_in, w_out, ln_g, ln_b, *cast_weights)


def _rope(t, cos, sin_signed):
    return t * cos + pltpu.roll(t, HEAD_DIM // 2, 1) * sin_signed


def _gelu(t):
    return 0.5 * t * (1.0 + lax.erf(t * (0.5 ** 0.5)))


def _mixer_kernel(x_ref, xprev_ref, win_ref, wout_ref, ropeq_ref, ropek_ref, decay_ref, xi_ref, zeta_ref,
                  gn_g_ref, gn_b_ref, zln_g_ref, zln_b_ref, ws_ref, bs_ref, ln_g_ref, ln_b_ref,
                  o_ref, state_ref, mix_ref, *, gamma_block, n_seq):
    i = pl.program_id(0)

    def start_current(first_step, finish):
        _mix_tile(x_ref, win_ref, ropeq_ref, ropek_ref, decay_ref, xi_ref, zeta_ref,
                  gn_g_ref, gn_b_ref, zln_g_ref, zln_b_ref, ws_ref, bs_ref, state_ref,
                  mix_ref, finish,
                  first_step=first_step, first_of_sequence=i % n_seq == 0, gamma_block=gamma_block)

    def finish_previous():
        norms = []
        for lo in range(0, MIX_TOKENS, MIX_OUT_ROWS):
            y = _dot(mix_ref[lo:lo + MIX_OUT_ROWS, :], wout_ref[...])
            norms += _row_group_norms(lambda rows: DEEPNORM_ALPHA * xprev_ref[rows, :], y, lo,
                                      ln_g_ref, ln_b_ref, o_ref)
        return norms

    _skewed_steps(start_current, finish_previous)


def _mix_tile(x_ref, win_ref, ropeq_ref, ropek_ref, decay_ref, xi_ref, zeta_ref,
              gn_g_ref, gn_b_ref, zln_g_ref, zln_b_ref, ws_ref, bs_ref, state_ref, mix_ref, finish,
              *, first_step, first_of_sequence, gamma_block):
    xb = x_ref[...].astype(BF16)
    blocks = range(MIX_TOKENS // RET_BLOCK)
    block_rows = [slice(c * RET_BLOCK, (c + 1) * RET_BLOCK) for c in blocks]
    half = RET_WIDTH // 2
    contract_last = (((1,), (1,)), ((), ()))
    contract_first = (((0,), (0,)), ((), ()))

    def proj(k):
        return _dot(xb, win_ref[:, k * RET_WIDTH:(k + 1) * RET_WIDTH])

    def proj_half(k, part):
        lo = k * RET_WIDTH + part * half
        return _dot(xb, win_ref[:, lo:lo + half])

    def unit_cols(halves, i):
        return halves[i // 2][:, (i % 2) * HEAD_DIM:(i % 2 + 1) * HEAD_DIM]

    def head_scores(h):
        cols = slice(h * HEAD_DIM, (h + 1) * HEAD_DIM)
        qr, scores, kv = [], [], []
        for rows in block_rows:
            qc = _rope(q[rows, cols], ropeq_ref[rows, :HEAD_DIM], ropeq_ref[rows, HEAD_DIM:]).astype(BF16)
            kc = _rope(k[rows, cols], ropek_ref[rows, :HEAD_DIM], ropek_ref[rows, HEAD_DIM:]).astype(BF16)
            qr.append(qc)
            scores.append(lax.dot_general(qc, kc, contract_last, preferred_element_type=F32))
            kv.append(lax.dot_general(kc, (v[rows, cols] * zeta_ref[h]).astype(BF16), contract_first,
                                      preferred_element_type=F32))
        return qr, scores, kv

    def head_retention(h, qr, scores, kv):
        cols = slice(h * HEAD_DIM, (h + 1) * HEAD_DIM)
        if first_step:
            state = jnp.zeros((HEAD_DIM, HEAD_DIM), F32)
        else:
            state = state_ref[h]
            state = jnp.where(first_of_sequence, jnp.zeros_like(state), state)
        ret = []
        for c in blocks:
            decayed = (scores[c] * decay_ref[h]).astype(BF16)
            intra = _dot(decayed, v[block_rows[c], cols].astype(BF16))
            cross = _dot(qr[c], state.astype(BF16)) * xi_ref[h]
            ret.append(intra + cross)
            state = gamma_block[h] * state + kv[c]
        state_ref[h] = state
        return ret

    def head_gate(h, ret, g_halves):
        cols = slice(h * HEAD_DIM, (h + 1) * HEAD_DIM)
        gate_cols = unit_cols(g_halves, h)
        for c in blocks:
            gate = gate_cols[block_rows[c], :]
            normed = _layer_norm(ret[c], gn_g_ref[:, cols], gn_b_ref[:, cols])
            mix_ref[block_rows[c], cols] = (gate * jax.nn.sigmoid(gate) * normed).astype(BF16)

    n_sub = MIX_TOKENS // GMLP_CHUNK
    row_id = lax.broadcasted_iota(jnp.int32, (GMLP_CHUNK, GMLP_CHUNK), 0)
    col_id = lax.broadcasted_iota(jnp.int32, (GMLP_CHUNK, GMLP_CHUNK), 1)
    causal = row_id >= col_id

    def group_mix(gi, z_halves):
        cols = slice(gi * HEAD_DIM, (gi + 1) * HEAD_DIM)
        zn = _layer_norm(_gelu(unit_cols(z_halves, gi)), zln_g_ref[:, cols], zln_b_ref[:, cols]).astype(BF16)
        zcat = jnp.concatenate([zn[s * GMLP_CHUNK:(s + 1) * GMLP_CHUNK] for s in range(n_sub)], axis=1)
        w = jnp.where(causal, ws_ref[gi], 0.0).astype(BF16)
        return _dot(w, zcat)

    def group_gate(gi, mixed, u_halves):
        u_cols = unit_cols(u_halves, gi)
        for s in range(n_sub):
            rows = slice(s * GMLP_CHUNK, (s + 1) * GMLP_CHUNK)
            m = mixed[:, s * GMLP_CHUNK:(s + 1) * GMLP_CHUNK] + bs_ref[gi]
            mix_ref[rows, RET_WIDTH + gi * HEAD_DIM:RET_WIDTH + (gi + 1) * HEAD_DIM] = (
                _gelu(u_cols[rows, :]) * m).astype(BF16)

    q, k, v = proj(0), proj(1), proj(2)
    for norm_rows in finish():
        norm_rows()

    g_halves = [proj_half(3, 0), proj_half(3, 1)]
    fillers = [(5, 0), (5, 1), (4, 0), (4, 1)]
    slabs = []
    for h in range(RET_HEADS):
        s = head_scores(h)
        slabs.append(proj_half(*fillers[h]))
        r = head_retention(h, *s)
        head_gate(h, r, g_halves)
    z_halves, u_halves = slabs[:2], slabs[2:]
    mixed = [group_mix(gi, z_halves) for gi in range(GMLP_GROUPS)]
    for gi in range(GMLP_GROUPS):
        group_gate(gi, mixed[gi], u_halves)


def _retention_tables(block):
    log_gamma = np.log1p(-(2.0 ** (-5.0 - np.arange(RET_HEADS, dtype=np.float64))))
    idx = np.arange(block, dtype=np.float64)
    rel = idx[:, None] - idx[None, :]
    decay = np.where(rel[None] >= 0, np.exp(np.maximum(rel, 0.0)[None] * log_gamma[:, None, None]), 0.0)
    xi = np.exp((idx + 1.0)[None, :] * log_gamma[:, None])
    zeta = np.exp((block - 1.0 - idx)[None, :] * log_gamma[:, None])
    widen = lambda t: np.broadcast_to(t[:, :, None], (RET_HEADS, block, HEAD_DIM))
    return tuple(jnp.asarray(t, F32) for t in (decay, widen(xi), widen(zeta)))


def _rope_tables(seq):
    inv_freq = ROPE_BASE ** (-np.arange(0, HEAD_DIM, 2, dtype=np.float64) / HEAD_DIM)
    ang = np.arange(seq, dtype=np.float64)[:, None] * inv_freq[None, :]
    cos, sin = np.cos(ang), np.sin(ang)
    table = np.concatenate([cos, cos, -sin, sin], axis=-1)
    return jnp.asarray(table, F32), jnp.asarray(table * (HEAD_DIM ** -0.5), F32)


def _mixer_call(x, batch, seq, w_in, w_out, gn_g, gn_b, zln_g, zln_b, w_s, b_s, ln_g, ln_b):
    n_seq = seq // MIX_TOKENS
    n_tiles = batch * n_seq
    rope_q, rope_k = _rope_tables(seq)
    decay, xi, zeta = _retention_tables(RET_BLOCK)
    gamma_block = tuple(float((1.0 - 2.0 ** (-5.0 - h)) ** RET_BLOCK) for h in range(RET_HEADS))
    bs_full = jnp.broadcast_to(b_s[:, :, None], (GMLP_GROUPS, GMLP_CHUNK, HEAD_DIM))

    def resident(shape):
        zeros = (0,) * len(shape)
        return pl.BlockSpec(shape, lambda i: zeros, pipeline_mode=pl.Buffered(1))

    tok_in = pl.BlockSpec((MIX_TOKENS, D_MODEL), lambda i: (jnp.minimum(i, n_tiles - 1), 0))
    previous = lambda i: (jnp.maximum(i - 1, 0), 0)
    tok_prev = pl.BlockSpec((MIX_TOKENS, D_MODEL), previous)
    tok_out = pl.BlockSpec((MIX_TOKENS, D_MODEL), previous)
    pos = pl.BlockSpec((MIX_TOKENS, 2 * HEAD_DIM), lambda i: (jnp.minimum(i, n_tiles - 1) % n_seq, 0))
    return pl.pallas_call(
        functools.partial(_mixer_kernel, gamma_block=gamma_block, n_seq=n_seq),
        out_shape=jax.ShapeDtypeStruct(x.shape, F32),
        grid=(n_tiles + 1,),
        in_specs=[
            tok_in, tok_prev,
            resident((D_MODEL, IN_WIDTH)),
            resident((D_MODEL, D_MODEL)),
            pos, pos,
            resident((RET_HEADS, RET_BLOCK, RET_BLOCK)),
            resident((RET_HEADS, RET_BLOCK, HEAD_DIM)),
            resident((RET_HEADS, RET_BLOCK, HEAD_DIM)),
            resident((1, RET_WIDTH)), resident((1, RET_WIDTH)),
            resident((1, GMLP_WIDTH)), resident((1, GMLP_WIDTH)),
            resident((GMLP_GROUPS, GMLP_CHUNK, GMLP_CHUNK)),
            resident((GMLP_GROUPS, GMLP_CHUNK, HEAD_DIM)),
            resident((1, D_MODEL)), resident((1, D_MODEL)),
        ],
        out_specs=tok_out,
        scratch_shapes=[
            pltpu.VMEM((RET_HEADS, HEAD_DIM, HEAD_DIM), F32),
            pltpu.VMEM((MIX_TOKENS, D_MODEL), BF16),
        ],
        compiler_params=pltpu.CompilerParams(
            dimension_semantics=("arbitrary",), vmem_limit_bytes=VMEM_LIMIT_BYTES),
        name="mixer_ln",
    )(x, x, w_in, w_out, rope_q, rope_k, decay, xi, zeta, gn_g, gn_b, zln_g, zln_b, w_s, bs_full, ln_g, ln_b)


def kernel(x, ffn1_w_in, ffn1_w_out, ln1_g, ln1_b, mix_w_in, ret_gn_g, ret_gn_b, gmlp_ln_g, gmlp_ln_b,
           gmlp_w_s, gmlp_b_s, mix_w_out, ln2_g, ln2_b, ffn2_w_in, ffn2_w_out, ln3_g, ln3_b):
    batch, seq, d = x.shape
    assert d == D_MODEL and seq % MIX_TOKENS == 0 and (batch * seq) % FFN_TOKENS == 0
    assert MIX_TOKENS % RET_BLOCK == 0 and RET_BLOCK % CHUNK == 0 and MIX_TOKENS % GMLP_CHUNK == 0
    row = lambda t: t.reshape(1, -1)
    h = x.reshape(batch * seq, d)
    for l in range(DEPTH):
        h, mix_in, mix_out, ffn2_in, ffn2_out = _ffn_call(
            h, ffn1_w_in[l].astype(BF16), ffn1_w_out[l].astype(BF16), row(ln1_g[l]), row(ln1_b[l]),
            cast_weights=(mix_w_in[l], mix_w_out[l], ffn2_w_in[l], ffn2_w_out[l]))
        h = _mixer_call(h, batch, seq, mix_in, mix_out,
                        row(ret_gn_g[l]), row(ret_gn_b[l]), row(gmlp_ln_g[l]), row(gmlp_ln_b[l]),
                        gmlp_w_s[l], gmlp_b_s[l], row(ln2_g[l]), row(ln2_b[l]))
        h, = _ffn_call(h, ffn2_in, ffn2_out, row(ln3_g[l]), row(ln3_b[l]))
    return h.reshape(batch, seq, d)
```

```python
import functools

import jax
import jax.numpy as jnp
import numpy as np
from jax import lax
from jax.experimental import pallas as pl
from jax.experimental.pallas import tpu as pltpu

F32 = jnp.float32
BF16 = jnp.bfloat16

D_MODEL = 1024
DEPTH = 1
CHUNK = 64
RET_HEADS = 4
HEAD_DIM = 128
RET_WIDTH = RET_HEADS * HEAD_DIM
GMLP_GROUPS = 4
GMLP_WIDTH = GMLP_GROUPS * HEAD_DIM
GMLP_CHUNK = 128
IN_WIDTH = 4 * RET_WIDTH + 2 * GMLP_WIDTH
D_FF = 2816
ROPE_BASE = 10000.0
LN_EPS = 1e-5
DEEPNORM_ALPHA = (2.0 * DEPTH) ** 0.25

LANES = 128
SUBLANES = 8
BF16_ROWS = 16
MXU_DIM = 256
VMEM_LIMIT_BYTES = 56 * 1024 * 1024

FFN_TOKENS = 1024
FFN_CHUNK = MXU_DIM
MIX_TOKENS = 1024
MIX_OUT_ROWS = 256
RET_BLOCK = 256
NORM_ROWS = SUBLANES
FFN_OUT_SPLITS = (256, 512, 768)


def _layer_norm(r, g, b):
    mu = jnp.mean(r, axis=-1, keepdims=True)
    d = r - mu
    var = jnp.mean(d * d, axis=-1, keepdims=True)
    return d * lax.rsqrt(var + LN_EPS) * g + b


def _dot(a, b):
    return jnp.dot(a, b, preferred_element_type=F32)


def _skewed_steps(start_current, finish_previous):
    i = pl.program_id(0)
    last = pl.num_programs(0) - 1

    @pl.when(i == 0)
    def _():
        start_current(True, lambda: [])

    @pl.when(jnp.logical_and(i > 0, i < last))
    def _():
        start_current(False, finish_previous)

    @pl.when(i == last)
    def _():
        for norm_rows in finish_previous():
            norm_rows()


def _row_group_norms(resid, y, first_row, g_ref, b_ref, o_ref):
    def norm_rows(lo):
        rows = slice(first_row + lo, first_row + lo + NORM_ROWS)
        o_ref[rows, :] = _layer_norm(resid(rows) + y[lo:lo + NORM_ROWS, :], g_ref[...], b_ref[...])
    return [functools.partial(norm_rows, lo) for lo in range(0, y.shape[0], NORM_ROWS)]


def _ffn_kernel(n_cast, x_ref, win_ref, wout_ref, g_ref, b_ref, *refs):
    cast_src, o_ref, cast_dst, act_ref = refs[:n_cast], refs[n_cast], refs[n_cast + 1:-1], refs[-1]
    xb = x_ref[...].astype(BF16)
    for j in range(D_FF // FFN_CHUNK):
        lo = j * FFN_CHUNK
        gate = _dot(xb, win_ref[:, lo:lo + FFN_CHUNK])
        up = _dot(xb, win_ref[:, D_FF + lo:D_FF + lo + FFN_CHUNK])
        act_ref[:, lo:lo + FFN_CHUNK] = (gate * jax.nn.sigmoid(gate) * up).astype(BF16)
    for src_ref, dst_ref in zip(cast_src, cast_dst):
        dst_ref[...] = src_ref[...].astype(BF16)
    for lo, hi in zip((0,) + FFN_OUT_SPLITS, FFN_OUT_SPLITS + (FFN_TOKENS,)):
        rows = slice(lo, hi)
        y = _dot(act_ref[rows, :], wout_ref[...])
        o_ref[rows, :] = _layer_norm(DEEPNORM_ALPHA * x_ref[rows, :] + 0.5 * y, g_ref[...], b_ref[...])


def _row_chunk_spec(shape, n_steps):
    rows = next(r for r in range(BF16_ROWS, shape[0] + 1, BF16_ROWS)
                if shape[0] % r == 0 and shape[0] // r <= n_steps)
    last = shape[0] // rows - 1
    return pl.BlockSpec((rows, shape[1]), lambda i: (jnp.minimum(i, last), 0))


def _ffn_call(x, w_in, w_out, ln_g, ln_b, cast_weights=()):
    n = x.shape[0]
    n_steps = n // FFN_TOKENS
    const = lambda i: (0, 0)
    resident = functools.partial(pl.BlockSpec, index_map=const, pipeline_mode=pl.Buffered(1))
    tile = lambda: pl.BlockSpec((FFN_TOKENS, D_MODEL), lambda i: (i, 0))
    chunks = lambda: [_row_chunk_spec(w.shape, n_steps) for w in cast_weights]
    return pl.pallas_call(
        functools.partial(_ffn_kernel, len(cast_weights)),
        out_shape=[jax.ShapeDtypeStruct((n, D_MODEL), F32)] +
                  [jax.ShapeDtypeStruct(w.shape, BF16) for w in cast_weights],
        grid=(n_steps,),
        in_specs=[
            tile(),
            resident((D_MODEL, 2 * D_FF)),
            resident((D_FF, D_MODEL)),
            resident((1, D_MODEL)),
            resident((1, D_MODEL)),
        ] + chunks(),
        out_specs=[tile()] + chunks(),
        scratch_shapes=[pltpu.VMEM((FFN_TOKENS, D_FF), BF16)],
        compiler_params=pltpu.CompilerParams(
            dimension_semantics=("arbitrary",), vmem_limit_bytes=VMEM_LIMIT_BYTES),
        name="ffn_ln",
    )(x, w_in, w_out, ln_g, ln_b, *cast_weights)


def _rope(t, cos, sin_signed):
    return t * cos + pltpu.roll(t, HEAD_DIM // 2, 1) * sin_signed


def _gelu(t):
    return 0.5 * t * (1.0 + lax.erf(t * (0.5 ** 0.5)))


def _mixer_kernel(x_ref, xprev_ref, win_ref, wout_ref, ropeq_ref, ropek_ref, decay_ref, xi_ref, zeta_ref,
                  gn_g_ref, gn_b_ref, zln_g_ref, zln_b_ref, ws_ref, bs_ref, ln_g_ref, ln_b_ref,
                  o_ref, state_ref, mix_ref, *, gamma_block, n_seq):
    i = pl.program_id(0)

    def start_current(first_step, finish):
        _mix_tile(x_ref, win_ref, ropeq_ref, ropek_ref, decay_ref, xi_ref, zeta_ref,
                  gn_g_ref, gn_b_ref, zln_g_ref, zln_b_ref, ws_ref, bs_ref, state_ref,
                  mix_ref, finish,
                  first_step=first_step, first_of_sequence=i % n_seq == 0, gamma_block=gamma_block)

    def finish_previous():
        norms = []
        for lo in range(0, MIX_TOKENS, MIX_OUT_ROWS):
            y = _dot(mix_ref[lo:lo + MIX_OUT_ROWS, :], wout_ref[...])
            norms += _row_group_norms(lambda rows: DEEPNORM_ALPHA * xprev_ref[rows, :], y, lo,
                                      ln_g_ref, ln_b_ref, o_ref)
        return norms

    _skewed_steps(start_current, finish_previous)


def _mix_tile(x_ref, win_ref, ropeq_ref, ropek_ref, decay_ref, xi_ref, zeta_ref,
              gn_g_ref, gn_b_ref, zln_g_ref, zln_b_ref, ws_ref, bs_ref, state_ref, mix_ref, finish,
              *, first_step, first_of_sequence, gamma_block):
    xb = x_ref[...].astype(BF16)
    blocks = range(MIX_TOKENS // RET_BLOCK)
    block_rows = [slice(c * RET_BLOCK, (c + 1) * RET_BLOCK) for c in blocks]
    half = RET_WIDTH // 2
    contract_last = (((1,), (1,)), ((), ()))
    contract_first = (((0,), (0,)), ((), ()))

    def proj(k):
        return _dot(xb, win_ref[:, k * RET_WIDTH:(k + 1) * RET_WIDTH])

    def proj_half(k, part):
        lo = k * RET_WIDTH + part * half
        return _dot(xb, win_ref[:, lo:lo + half])

    def unit_cols(halves, i):
        return halves[i // 2][:, (i % 2) * HEAD_DIM:(i % 2 + 1) * HEAD_DIM]

    def head_scores(h):
        cols = slice(h * HEAD_DIM, (h + 1) * HEAD_DIM)
        qr, scores, kv = [], [], []
        for rows in block_rows:
            qc = _rope(q[rows, cols], ropeq_ref[rows, :HEAD_DIM], ropeq_ref[rows, HEAD_DIM:]).astype(BF16)
            kc = _rope(k[rows, cols], ropek_ref[rows, :HEAD_DIM], ropek_ref[rows, HEAD_DIM:]).astype(BF16)
            qr.append(qc)
            scores.append(lax.dot_general(qc, kc, contract_last, preferred_element_type=F32))
            kv.append(lax.dot_general(kc, (v[rows, cols] * zeta_ref[h]).astype(BF16), contract_first,
                                      preferred_element_type=F32))
        return qr, scores, kv

    def head_retention(h, qr, scores, kv):
        cols = slice(h * HEAD_DIM, (h + 1) * HEAD_DIM)
        if first_step:
            state = jnp.zeros((HEAD_DIM, HEAD_DIM), F32)
        else:
            state = state_ref[h]
            state = jnp.where(first_of_sequence, jnp.zeros_like(state), state)
        ret = []
        for c in blocks:
            decayed = (scores[c] * decay_ref[h]).astype(BF16)
            intra = _dot(decayed, v[block_rows[c], cols].astype(BF16))
            cross = _dot(qr[c], state.astype(BF16)) * xi_ref[h]
            ret.append(intra + cross)
            state = gamma_block[h] * state + kv[c]
        state_ref[h] = state
        return ret

    def head_gate(h, ret, g_halves):
        cols = slice(h * HEAD_DIM, (h + 1) * HEAD_DIM)
        gate_cols = unit_cols(g_halves, h)
        for c in blocks:
            gate = gate_cols[block_rows[c], :]
            normed = _layer_norm(ret[c], gn_g_ref[:, cols], gn_b_ref[:, cols])
            mix_ref[block_rows[c], cols] = (gate * jax.nn.sigmoid(gate) * normed).astype(BF16)

    n_sub = MIX_TOKENS // GMLP_CHUNK
    row_id = lax.broadcasted_iota(jnp.int32, (GMLP_CHUNK, GMLP_CHUNK), 0)
    col_id = lax.broadcasted_iota(jnp.int32, (GMLP_CHUNK, GMLP_CHUNK), 1)
    causal = row_id >= col_id

    def group_mix(gi, z_halves):
        cols = slice(gi * HEAD_DIM, (gi + 1) * HEAD_DIM)
        zn = _layer_norm(_gelu(unit_cols(z_halves, gi)), zln_g_ref[:, cols], zln_b_ref[:, cols]).astype(BF16)
        zcat = jnp.concatenate([zn[s * GMLP_CHUNK:(s + 1) * GMLP_CHUNK] for s in range(n_sub)], axis=1)
        w = jnp.where(causal, ws_ref[gi], 0.0).astype(BF16)
        return _dot(w, zcat)

    def group_gate(gi, mixed, u_halves):
        u_cols = unit_cols(u_halves, gi)
        for s in range(n_sub):
            rows = slice(s * GMLP_CHUNK, (s + 1) * GMLP_CHUNK)
            m = mixed[:, s * GMLP_CHUNK:(s + 1) * GMLP_CHUNK] + bs_ref[gi]
            mix_ref[rows, RET_WIDTH + gi * HEAD_DIM:RET_WIDTH + (gi + 1) * HEAD_DIM] = (
                _gelu(u_cols[rows, :]) * m).astype(BF16)

    q, k, v = proj(0), proj(1), proj(2)
    for norm_rows in finish():
        norm_rows()

    g_halves = [proj_half(3, 0), proj_half(3, 1)]
    fillers = [(5, 0), (5, 1), (4, 0), (4, 1)]
    slabs = []
    for h in range(RET_HEADS):
        s = head_scores(h)
        slabs.append(proj_half(*fillers[h]))
        r = head_retention(h, *s)
        head_gate(h, r, g_halves)
    z_halves, u_halves = slabs[:2], slabs[2:]
    mixed = [group_mix(gi, z_halves) for gi in range(GMLP_GROUPS)]
    for gi in range(GMLP_GROUPS):
        group_gate(gi, mixed[gi], u_halves)


def _retention_tables(block):
    log_gamma = np.log1p(-(2.0 ** (-5.0 - np.arange(RET_HEADS, dtype=np.float64))))
    idx = np.arange(block, dtype=np.float64)
    rel = idx[:, None] - idx[None, :]
    decay = np.where(rel[None] >= 0, np.exp(np.maximum(rel, 0.0)[None] * log_gamma[:, None, None]), 0.0)
    xi = np.exp((idx + 1.0)[None, :] * log_gamma[:, None])
    zeta = np.exp((block - 1.0 - idx)[None, :] * log_gamma[:, None])
    widen = lambda t: np.broadcast_to(t[:, :, None], (RET_HEADS, block, HEAD_DIM))
    return tuple(jnp.asarray(t, F32) for t in (decay, widen(xi), widen(zeta)))


def _rope_tables(seq):
    inv_freq = ROPE_BASE ** (-np.arange(0, HEAD_DIM, 2, dtype=np.float64) / HEAD_DIM)
    ang = np.arange(seq, dtype=np.float64)[:, None] * inv_freq[None, :]
    cos, sin = np.cos(ang), np.sin(ang)
    table = np.concatenate([cos, cos, -sin, sin], axis=-1)
    return jnp.asarray(table, F32), jnp.asarray(table * (HEAD_DIM ** -0.5), F32)


def _mixer_call(x, batch, seq, w_in, w_out, gn_g, gn_b, zln_g, zln_b, w_s, b_s, ln_g, ln_b):
    n_seq = seq // MIX_TOKENS
    n_tiles = batch * n_seq
    rope_q, rope_k = _rope_tables(seq)
    decay, xi, zeta = _retention_tables(RET_BLOCK)
    gamma_block = tuple(float((1.0 - 2.0 ** (-5.0 - h)) ** RET_BLOCK) for h in range(RET_HEADS))
    bs_full = jnp.broadcast_to(b_s[:, :, None], (GMLP_GROUPS, GMLP_CHUNK, HEAD_DIM))

    def resident(shape):
        zeros = (0,) * len(shape)
        return pl.BlockSpec(shape, lambda i: zeros, pipeline_mode=pl.Buffered(1))

    tok_in = pl.BlockSpec((MIX_TOKENS, D_MODEL), lambda i: (jnp.minimum(i, n_tiles - 1), 0))
    previous = lambda i: (jnp.maximum(i - 1, 0), 0)
    tok_prev = pl.BlockSpec((MIX_TOKENS, D_MODEL), previous)
    tok_out = pl.BlockSpec((MIX_TOKENS, D_MODEL), previous)
    pos = pl.BlockSpec((MIX_TOKENS, 2 * HEAD_DIM), lambda i: (jnp.minimum(i, n_tiles - 1) % n_seq, 0))
    return pl.pallas_call(
        functools.partial(_mixer_kernel, gamma_block=gamma_block, n_seq=n_seq),
        out_shape=jax.ShapeDtypeStruct(x.shape, F32),
        grid=(n_tiles + 1,),
        in_specs=[
            tok_in, tok_prev,
            resident((D_MODEL, IN_WIDTH)),
            resident((D_MODEL, D_MODEL)),
            pos, pos,
            resident((RET_HEADS, RET_BLOCK, RET_BLOCK)),
            resident((RET_HEADS, RET_BLOCK, HEAD_DIM)),
            resident((RET_HEADS, RET_BLOCK, HEAD_DIM)),
            resident((1, RET_WIDTH)), resident((1, RET_WIDTH)),
            resident((1, GMLP_WIDTH)), resident((1, GMLP_WIDTH)),
            resident((GMLP_GROUPS, GMLP_CHUNK, GMLP_CHUNK)),
            resident((GMLP_GROUPS, GMLP_CHUNK, HEAD_DIM)),
            resident((1, D_MODEL)), resident((1, D_MODEL)),
        ],
        out_specs=tok_out,
        scratch_shapes=[
            pltpu.VMEM((RET_HEADS, HEAD_DIM, HEAD_DIM), F32),
            pltpu.VMEM((MIX_TOKENS, D_MODEL), BF16),
        ],
        compiler_params=pltpu.CompilerParams(
            dimension_semantics=("arbitrary",), vmem_limit_bytes=VMEM_LIMIT_BYTES),
        name="mixer_ln",
    )(x, x, w_in, w_out, rope_q, rope_k, decay, xi, zeta, gn_g, gn_b, zln_g, zln_b, w_s, bs_full, ln_g, ln_b)


def kernel(x, ffn1_w_in, ffn1_w_out, ln1_g, ln1_b, mix_w_in, ret_gn_g, ret_gn_b, gmlp_ln_g, gmlp_ln_b,
           gmlp_w_s, gmlp_b_s, mix_w_out, ln2_g, ln2_b, ffn2_w_in, ffn2_w_out, ln3_g, ln3_b):
    batch, seq, d = x.shape
    assert d == D_MODEL and seq % MIX_TOKENS == 0 and (batch * seq) % FFN_TOKENS == 0
    assert MIX_TOKENS % RET_BLOCK == 0 and RET_BLOCK % CHUNK == 0 and MIX_TOKENS % GMLP_CHUNK == 0
    row = lambda t: t.reshape(1, -1)
    h = x.reshape(batch * seq, d)
    for l in range(DEPTH):
        h, mix_in, mix_out, ffn2_in, ffn2_out = _ffn_call(
            h, ffn1_w_in[l].astype(BF16), ffn1_w_out[l].astype(BF16), row(ln1_g[l]), row(ln1_b[l]),
            cast_weights=(mix_w_in[l], mix_w_out[l], ffn2_w_in[l], ffn2_w_out[l]))
        h = _mixer_call(h, batch, seq, mix_in, mix_out,
                        row(ret_gn_g[l]), row(ret_gn_b[l]), row(gmlp_ln_g[l]), row(gmlp_ln_b[l]),
                        gmlp_w_s[l], gmlp_b_s[l], row(ln2_g[l]), row(ln2_b[l]))
        h, = _ffn_call(h, ffn2_in, ffn2_out, row(ln3_g[l]), row(ln3_b[l]))
    return h.reshape(batch, seq, d)
```

```python
import functools

import jax
import jax.numpy as jnp
import numpy as np
from jax import lax
from jax.experimental import pallas as pl
from jax.experimental.pallas import tpu as pltpu

F32 = jnp.float32
BF16 = jnp.bfloat16

D_MODEL = 1024
DEPTH = 1
CHUNK = 64
RET_HEADS = 4
HEAD_DIM = 128
RET_WIDTH = RET_HEADS * HEAD_DIM
GMLP_GROUPS = 4
GMLP_WIDTH = GMLP_GROUPS * HEAD_DIM
GMLP_CHUNK = 128
IN_WIDTH = 4 * RET_WIDTH + 2 * GMLP_WIDTH
D_FF = 2816
ROPE_BASE = 10000.0
LN_EPS = 1e-5
DEEPNORM_ALPHA = (2.0 * DEPTH) ** 0.25

SUBLANES = 8
BF16_ROWS = 16
MXU_DIM = 256
VMEM_LIMIT_BYTES = 56 * 1024 * 1024

FFN_TOKENS = 1024
FFN_CHUNK = MXU_DIM
MIX_TOKENS = 512
MIX_OUT_ROWS = 256
RET_BLOCK = 256
NORM_ROWS = SUBLANES
FFN_OUT_SPLITS = (256, 512, 768)


def _layer_norm(r, g, b):
    mu = jnp.mean(r, axis=-1, keepdims=True)
    d = r - mu
    var = jnp.mean(d * d, axis=-1, keepdims=True)
    return d * lax.rsqrt(var + LN_EPS) * g + b


def _dot(a, b):
    return jnp.dot(a, b, preferred_element_type=F32)


def _skewed_steps(start_current, finish_previous):
    i = pl.program_id(0)
    last = pl.num_programs(0) - 1

    @pl.when(i == 0)
    def _():
        start_current(True, lambda: [])

    @pl.when(jnp.logical_and(i > 0, i < last))
    def _():
        start_current(False, finish_previous)

    @pl.when(i == last)
    def _():
        for norm_rows in finish_previous():
            norm_rows()


def _row_group_norms(resid, y, first_row, g_ref, b_ref, o_ref):
    def norm_rows(lo):
        rows = slice(first_row + lo, first_row + lo + NORM_ROWS)
        o_ref[rows, :] = _layer_norm(resid(rows) + y[lo:lo + NORM_ROWS, :], g_ref[...], b_ref[...])
    return [functools.partial(norm_rows, lo) for lo in range(0, y.shape[0], NORM_ROWS)]


def _ffn_kernel(n_cast, x_ref, win_ref, wout_ref, g_ref, b_ref, *refs):
    cast_src, o_ref, cast_dst, act_ref = refs[:n_cast], refs[n_cast], refs[n_cast + 1:-1], refs[-1]
    xb = x_ref[...].astype(BF16)
    for j in range(D_FF // FFN_CHUNK):
        lo = j * FFN_CHUNK
        gate = _dot(xb, win_ref[:, lo:lo + FFN_CHUNK])
        up = _dot(xb, win_ref[:, D_FF + lo:D_FF + lo + FFN_CHUNK])
        act_ref[:, lo:lo + FFN_CHUNK] = (gate * jax.nn.sigmoid(gate) * up).astype(BF16)
    for src_ref, dst_ref in zip(cast_src, cast_dst):
        dst_ref[...] = src_ref[...].astype(BF16)
    for lo, hi in zip((0,) + FFN_OUT_SPLITS, FFN_OUT_SPLITS + (FFN_TOKENS,)):
        rows = slice(lo, hi)
        y = _dot(act_ref[rows, :], wout_ref[...])
        o_ref[rows, :] = _layer_norm(DEEPNORM_ALPHA * x_ref[rows, :] + 0.5 * y, g_ref[...], b_ref[...])


def _row_chunk_spec(shape, n_steps):
    rows = next(r for r in range(BF16_ROWS, shape[0] + 1, BF16_ROWS)
                if shape[0] % r == 0 and shape[0] // r <= n_steps)
    last = shape[0] // rows - 1
    return pl.BlockSpec((rows, shape[1]), lambda i: (jnp.minimum(i, last), 0))


def _ffn_call(x, w_in, w_out, ln_g, ln_b, cast_weights=()):
    n = x.shape[0]
    n_steps = n // FFN_TOKENS
    const = lambda i: (0, 0)
    resident = functools.partial(pl.BlockSpec, index_map=const, pipeline_mode=pl.Buffered(1))
    tile = lambda: pl.BlockSpec((FFN_TOKENS, D_MODEL), lambda i: (i, 0))
    chunks = lambda: [_row_chunk_spec(w.shape, n_steps) for w in cast_weights]
    return pl.pallas_call(
        functools.partial(_ffn_kernel, len(cast_weights)),
        out_shape=[jax.ShapeDtypeStruct((n, D_MODEL), F32)] +
                  [jax.ShapeDtypeStruct(w.shape, BF16) for w in cast_weights],
        grid=(n_steps,),
        in_specs=[
            tile(),
            resident((D_MODEL, 2 * D_FF)),
            resident((D_FF, D_MODEL)),
            resident((1, D_MODEL)),
            resident((1, D_MODEL)),
        ] + chunks(),
        out_specs=[tile()] + chunks(),
        scratch_shapes=[pltpu.VMEM((FFN_TOKENS, D_FF), BF16)],
        compiler_params=pltpu.CompilerParams(
            dimension_semantics=("arbitrary",), vmem_limit_bytes=VMEM_LIMIT_BYTES),
        name="ffn_ln",
    )(x, w_in, w_out, ln_g, ln_b, *cast_weights)


def _rope(t, cos, sin_signed):
    return t * cos + pltpu.roll(t, HEAD_DIM // 2, 1) * sin_signed


def _gelu(t):
    return 0.5 * t * (1.0 + lax.erf(t * (0.5 ** 0.5)))


def _mixer_kernel(x_ref, xprev_ref, win_ref, wout_ref, ropeq_ref, ropek_ref, decay_ref, xi_ref, zeta_ref,
                  gn_g_ref, gn_b_ref, zln_g_ref, zln_b_ref, ws_ref, bs_ref, ln_g_ref, ln_b_ref,
                  o_ref, state_ref, mix_ref, *, gamma_block, n_seq):
    i = pl.program_id(0)

    def start_current(first_step, finish):
        _mix_tile(x_ref, win_ref, ropeq_ref, ropek_ref, decay_ref, xi_ref, zeta_ref,
                  gn_g_ref, gn_b_ref, zln_g_ref, zln_b_ref, ws_ref, bs_ref, state_ref,
                  mix_ref, finish,
                  first_step=first_step, first_of_sequence=i % n_seq == 0, gamma_block=gamma_block)

    def finish_previous():
        norms = []
        for lo in range(0, MIX_TOKENS, MIX_OUT_ROWS):
            y = _dot(mix_ref[lo:lo + MIX_OUT_ROWS, :], wout_ref[...])
            norms += _row_group_norms(lambda rows: DEEPNORM_ALPHA * xprev_ref[rows, :], y, lo,
                                      ln_g_ref, ln_b_ref, o_ref)
        return norms

    _skewed_steps(start_current, finish_previous)


def _mix_tile(x_ref, win_ref, ropeq_ref, ropek_ref, decay_ref, xi_ref, zeta_ref,
              gn_g_ref, gn_b_ref, zln_g_ref, zln_b_ref, ws_ref, bs_ref, state_ref, mix_ref, finish,
              *, first_step, first_of_sequence, gamma_block):
    xb = x_ref[...].astype(BF16)
    blocks = range(MIX_TOKENS // RET_BLOCK)
    block_rows = [slice(c * RET_BLOCK, (c + 1) * RET_BLOCK) for c in blocks]
    half = RET_WIDTH // 2
    contract_last = (((1,), (1,)), ((), ()))
    contract_first = (((0,), (0,)), ((), ()))

    def proj(k):
        return _dot(xb, win_ref[:, k * RET_WIDTH:(k + 1) * RET_WIDTH])

    def proj_half(k, part):
        lo = k * RET_WIDTH + part * half
        return _dot(xb, win_ref[:, lo:lo + half])

    def unit_cols(halves, i):
        return halves[i // 2][:, (i % 2) * HEAD_DIM:(i % 2 + 1) * HEAD_DIM]

    def head_scores(h):
        cols = slice(h * HEAD_DIM, (h + 1) * HEAD_DIM)
        qr, scores, kv = [], [], []
        for rows in block_rows:
            qc = _rope(q[rows, cols], ropeq_ref[rows, :HEAD_DIM], ropeq_ref[rows, HEAD_DIM:]).astype(BF16)
            kc = _rope(k[rows, cols], ropek_ref[rows, :HEAD_DIM], ropek_ref[rows, HEAD_DIM:]).astype(BF16)
            qr.append(qc)
            scores.append(lax.dot_general(qc, kc, contract_last, preferred_element_type=F32))
            kv.append(lax.dot_general(kc, (v[rows, cols] * zeta_ref[h]).astype(BF16), contract_first,
                                      preferred_element_type=F32))
        return qr, scores, kv

    def head_retention(h, qr, scores, kv):
        cols = slice(h * HEAD_DIM, (h + 1) * HEAD_DIM)
        if first_step:
            state = jnp.zeros((HEAD_DIM, HEAD_DIM), F32)
        else:
            state = state_ref[h]
            state = jnp.where(first_of_sequence, jnp.zeros_like(state), state)
        ret = []
        for c in blocks:
            decayed = (scores[c] * decay_ref[h]).astype(BF16)
            intra = _dot(decayed, v[block_rows[c], cols].astype(BF16))
            cross = _dot(qr[c], state.astype(BF16)) * xi_ref[h]
            ret.append(intra + cross)
            state = gamma_block[h] * state + kv[c]
        state_ref[h] = state
        return ret

    def head_gate(h, ret, g_halves):
        cols = slice(h * HEAD_DIM, (h + 1) * HEAD_DIM)
        gate_cols = unit_cols(g_halves, h)
        for c in blocks:
            gate = gate_cols[block_rows[c], :]
            normed = _layer_norm(ret[c], gn_g_ref[:, cols], gn_b_ref[:, cols])
            mix_ref[block_rows[c], cols] = (gate * jax.nn.sigmoid(gate) * normed).astype(BF16)

    n_sub = MIX_TOKENS // GMLP_CHUNK
    row_id = lax.broadcasted_iota(jnp.int32, (GMLP_CHUNK, GMLP_CHUNK), 0)
    col_id = lax.broadcasted_iota(jnp.int32, (GMLP_CHUNK, GMLP_CHUNK), 1)
    causal = row_id >= col_id

    def group_mix(gi, z_halves):
        cols = slice(gi * HEAD_DIM, (gi + 1) * HEAD_DIM)
        zn = _layer_norm(_gelu(unit_cols(z_halves, gi)), zln_g_ref[:, cols], zln_b_ref[:, cols]).astype(BF16)
        zcat = jnp.concatenate([zn[s * GMLP_CHUNK:(s + 1) * GMLP_CHUNK] for s in range(n_sub)], axis=1)
        w = jnp.where(causal, ws_ref[gi], 0.0).astype(BF16)
        return _dot(w, zcat)

    def group_gate(gi, mixed, u_halves):
        u_cols = unit_cols(u_halves, gi)
        for s in range(n_sub):
            rows = slice(s * GMLP_CHUNK, (s + 1) * GMLP_CHUNK)
            m = mixed[:, s * GMLP_CHUNK:(s + 1) * GMLP_CHUNK] + bs_ref[gi]
            mix_ref[rows, RET_WIDTH + gi * HEAD_DIM:RET_WIDTH + (gi + 1) * HEAD_DIM] = (
                _gelu(u_cols[rows, :]) * m).astype(BF16)

    q, k, v = proj(0), proj(1), proj(2)
    for norm_rows in finish():
        norm_rows()

    g_halves = [proj_half(3, 0), proj_half(3, 1)]
    fillers = [(5, 0), (5, 1), (4, 0), (4, 1)]
    slabs = []
    for h in range(RET_HEADS):
        s = head_scores(h)
        slabs.append(proj_half(*fillers[h]))
        r = head_retention(h, *s)
        head_gate(h, r, g_halves)
    z_halves, u_halves = slabs[:2], slabs[2:]
    mixed = [group_mix(gi, z_halves) for gi in range(GMLP_GROUPS)]
    for gi in range(GMLP_GROUPS):
        group_gate(gi, mixed[gi], u_halves)


def _retention_tables(block):
    log_gamma = np.log1p(-(2.0 ** (-5.0 - np.arange(RET_HEADS, dtype=np.float64))))
    idx = np.arange(block, dtype=np.float64)
    rel = idx[:, None] - idx[None, :]
    decay = np.where(rel[None] >= 0, np.exp(np.maximum(rel, 0.0)[None] * log_gamma[:, None, None]), 0.0)
    xi = np.exp((idx + 1.0)[None, :] * log_gamma[:, None])
    zeta = np.exp((block - 1.0 - idx)[None, :] * log_gamma[:, None])
    widen = lambda t: np.broadcast_to(t[:, :, None], (RET_HEADS, block, HEAD_DIM))
    return tuple(jnp.asarray(t, F32) for t in (decay, widen(xi), widen(zeta)))


def _rope_tables(seq):
    inv_freq = ROPE_BASE ** (-np.arange(0, HEAD_DIM, 2, dtype=np.float64) / HEAD_DIM)
    ang = np.arange(seq, dtype=np.float64)[:, None] * inv_freq[None, :]
    cos, sin = np.cos(ang), np.sin(ang)
    table = np.concatenate([cos, cos, -sin, sin], axis=-1)
    return jnp.asarray(table, F32), jnp.asarray(table * (HEAD_DIM ** -0.5), F32)


def _mixer_call(x, batch, seq, w_in, w_out, gn_g, gn_b, zln_g, zln_b, w_s, b_s, ln_g, ln_b):
    n_seq = seq // MIX_TOKENS
    n_tiles = batch * n_seq
    rope_q, rope_k = _rope_tables(seq)
    decay, xi, zeta = _retention_tables(RET_BLOCK)
    gamma_block = tuple(float((1.0 - 2.0 ** (-5.0 - h)) ** RET_BLOCK) for h in range(RET_HEADS))
    bs_full = jnp.broadcast_to(b_s[:, :, None], (GMLP_GROUPS, GMLP_CHUNK, HEAD_DIM))

    def resident(shape):
        zeros = (0,) * len(shape)
        return pl.BlockSpec(shape, lambda i: zeros, pipeline_mode=pl.Buffered(1))

    tok_in = pl.BlockSpec((MIX_TOKENS, D_MODEL), lambda i: (jnp.minimum(i, n_tiles - 1), 0))
    previous = lambda i: (jnp.maximum(i - 1, 0), 0)
    tok_prev = pl.BlockSpec((MIX_TOKENS, D_MODEL), previous)
    tok_out = pl.BlockSpec((MIX_TOKENS, D_MODEL), previous)
    pos = pl.BlockSpec((MIX_TOKENS, 2 * HEAD_DIM), lambda i: (jnp.minimum(i, n_tiles - 1) % n_seq, 0))
    return pl.pallas_call(
        functools.partial(_mixer_kernel, gamma_block=gamma_block, n_seq=n_seq),
        out_shape=jax.ShapeDtypeStruct(x.shape, F32),
        grid=(n_tiles + 1,),
        in_specs=[
            tok_in, tok_prev,
            resident((D_MODEL, IN_WIDTH)),
            resident((D_MODEL, D_MODEL)),
            pos, pos,
            resident((RET_HEADS, RET_BLOCK, RET_BLOCK)),
            resident((RET_HEADS, RET_BLOCK, HEAD_DIM)),
            resident((RET_HEADS, RET_BLOCK, HEAD_DIM)),
            resident((1, RET_WIDTH)), resident((1, RET_WIDTH)),
            resident((1, GMLP_WIDTH)), resident((1, GMLP_WIDTH)),
            resident((GMLP_GROUPS, GMLP_CHUNK, GMLP_CHUNK)),
            resident((GMLP_GROUPS, GMLP_CHUNK, HEAD_DIM)),
            resident((1, D_MODEL)), resident((1, D_MODEL)),
        ],
        out_specs=tok_out,
        scratch_shapes=[
            pltpu.VMEM((RET_HEADS, HEAD_DIM, HEAD_DIM), F32),
            pltpu.VMEM((MIX_TOKENS, D_MODEL), BF16),
        ],
        compiler_params=pltpu.CompilerParams(
            dimension_semantics=("arbitrary",), vmem_limit_bytes=VMEM_LIMIT_BYTES),
        name="mixer_ln",
    )(x, x, w_in, w_out, rope_q, rope_k, decay, xi, zeta, gn_g, gn_b, zln_g, zln_b, w_s, bs_full, ln_g, ln_b)


def kernel(x, ffn1_w_in, ffn1_w_out, ln1_g, ln1_b, mix_w_in, ret_gn_g, ret_gn_b, gmlp_ln_g, gmlp_ln_b,
           gmlp_w_s, gmlp_b_s, mix_w_out, ln2_g, ln2_b, ffn2_w_in, ffn2_w_out, ln3_g, ln3_b):
    batch, seq, d = x.shape
    assert d == D_MODEL and seq % MIX_TOKENS == 0 and (batch * seq) % FFN_TOKENS == 0
    assert MIX_TOKENS % RET_BLOCK == 0 and RET_BLOCK % CHUNK == 0 and MIX_TOKENS % GMLP_CHUNK == 0
    row = lambda t: t.reshape(1, -1)
    h = x.reshape(batch * seq, d)
    for l in range(DEPTH):
        h, mix_in, mix_out, ffn2_in, ffn2_out = _ffn_call(
            h, ffn1_w_in[l].astype(BF16), ffn1_w_out[l].astype(BF16), row(ln1_g[l]), row(ln1_b[l]),
            cast_weights=(mix_w_in[l], mix_w_out[l], ffn2_w_in[l], ffn2_w_out[l]))
        h = _mixer_call(h, batch, seq, mix_in, mix_out,
                        row(ret_gn_g[l]), row(ret_gn_b[l]), row(gmlp_ln_g[l]), row(gmlp_ln_b[l]),
                        gmlp_w_s[l], gmlp_b_s[l], row(ln2_g[l]), row(ln2_b[l]))
        h, = _ffn_call(h, ffn2_in, ffn2_out, row(ln3_g[l]), row(ln3_b[l]))
    return h.reshape(batch, seq, d)
```

```python
import functools

import jax
import jax.numpy as jnp
import numpy as np
from jax import lax
from jax.experimental import pallas as pl
from jax.experimental.pallas import tpu as pltpu

F32 = jnp.float32
BF16 = jnp.bfloat16

D_MODEL = 1024
DEPTH = 1
CHUNK = 64
RET_HEADS = 4
HEAD_DIM = 128
RET_WIDTH = RET_HEADS * HEAD_DIM
GMLP_GROUPS = 4
GMLP_WIDTH = GMLP_GROUPS * HEAD_DIM
GMLP_CHUNK = 128
IN_WIDTH = 4 * RET_WIDTH + 2 * GMLP_WIDTH
D_FF = 2816
ROPE_BASE = 10000.0
LN_EPS = 1e-5
DEEPNORM_ALPHA = (2.0 * DEPTH) ** 0.25

SUBLANES = 8
BF16_ROWS = 16
MXU_DIM = 256
VMEM_LIMIT_BYTES = 56 * 1024 * 1024

FFN_TOKENS = 1024
FFN_CHUNK = MXU_DIM
MIX_TOKENS = 512
MIX_OUT_ROWS = 256
RET_BLOCK = 256
NORM_ROWS = SUBLANES
FFN_OUT_SPLITS = (208, 416, 624, 832)


def _layer_norm(r, g, b):
    mu = jnp.mean(r, axis=-1, keepdims=True)
    d = r - mu
    var = jnp.mean(d * d, axis=-1, keepdims=True)
    return d * lax.rsqrt(var + LN_EPS) * g + b


def _dot(a, b):
    return jnp.dot(a, b, preferred_element_type=F32)


def _skewed_steps(start_current, finish_previous):
    i = pl.program_id(0)
    last = pl.num_programs(0) - 1

    @pl.when(i == 0)
    def _():
        start_current(True, lambda: [])

    @pl.when(jnp.logical_and(i > 0, i < last))
    def _():
        start_current(False, finish_previous)

    @pl.when(i == last)
    def _():
        for norm_rows in finish_previous():
            norm_rows()


def _row_group_norms(resid, y, first_row, g_ref, b_ref, o_ref):
    def norm_rows(lo):
        rows = slice(first_row + lo, first_row + lo + NORM_ROWS)
        o_ref[rows, :] = _layer_norm(resid(rows) + y[lo:lo + NORM_ROWS, :], g_ref[...], b_ref[...])
    return [functools.partial(norm_rows, lo) for lo in range(0, y.shape[0], NORM_ROWS)]


def _ffn_kernel(n_cast, x_ref, win_ref, wout_ref, g_ref, b_ref, *refs):
    cast_src, o_ref, cast_dst, act_ref = refs[:n_cast], refs[n_cast], refs[n_cast + 1:-1], refs[-1]
    xb = x_ref[...].astype(BF16)
    for j in range(D_FF // FFN_CHUNK):
        lo = j * FFN_CHUNK
        gate = _dot(xb, win_ref[:, lo:lo + FFN_CHUNK])
        up = _dot(xb, win_ref[:, D_FF + lo:D_FF + lo + FFN_CHUNK])
        act_ref[:, lo:lo + FFN_CHUNK] = (gate * jax.nn.sigmoid(gate) * up).astype(BF16)
    for src_ref, dst_ref in zip(cast_src, cast_dst):
        dst_ref[...] = src_ref[...].astype(BF16)
    for lo, hi in zip((0,) + FFN_OUT_SPLITS, FFN_OUT_SPLITS + (FFN_TOKENS,)):
        rows = slice(lo, hi)
        y = _dot(act_ref[rows, :], wout_ref[...])
        o_ref[rows, :] = _layer_norm(DEEPNORM_ALPHA * x_ref[rows, :] + 0.5 * y, g_ref[...], b_ref[...])


def _row_chunk_spec(shape, n_steps):
    rows = next(r for r in range(BF16_ROWS, shape[0] + 1, BF16_ROWS)
                if shape[0] % r == 0 and shape[0] // r <= n_steps)
    last = shape[0] // rows - 1
    return pl.BlockSpec((rows, shape[1]), lambda i: (jnp.minimum(i, last), 0))


def _ffn_call(x, w_in, w_out, ln_g, ln_b, cast_weights=()):
    n = x.shape[0]
    n_steps = n // FFN_TOKENS
    const = lambda i: (0, 0)
    resident = functools.partial(pl.BlockSpec, index_map=const, pipeline_mode=pl.Buffered(1))
    tile = lambda: pl.BlockSpec((FFN_TOKENS, D_MODEL), lambda i: (i, 0))
    chunks = lambda: [_row_chunk_spec(w.shape, n_steps) for w in cast_weights]
    return pl.pallas_call(
        functools.partial(_ffn_kernel, len(cast_weights)),
        out_shape=[jax.ShapeDtypeStruct((n, D_MODEL), F32)] +
                  [jax.ShapeDtypeStruct(w.shape, BF16) for w in cast_weights],
        grid=(n_steps,),
        in_specs=[
            tile(),
            resident((D_MODEL, 2 * D_FF)),
            resident((D_FF, D_MODEL)),
            resident((1, D_MODEL)),
            resident((1, D_MODEL)),
        ] + chunks(),
        out_specs=[tile()] + chunks(),
        scratch_shapes=[pltpu.VMEM((FFN_TOKENS, D_FF), BF16)],
        compiler_params=pltpu.CompilerParams(
            dimension_semantics=("arbitrary",), vmem_limit_bytes=VMEM_LIMIT_BYTES),
        name="ffn_ln",
    )(x, w_in, w_out, ln_g, ln_b, *cast_weights)


def _rope(t, cos, sin_signed):
    return t * cos + pltpu.roll(t, HEAD_DIM // 2, 1) * sin_signed


def _gelu(t):
    return 0.5 * t * (1.0 + lax.erf(t * (0.5 ** 0.5)))


def _mixer_kernel(x_ref, xprev_ref, win_ref, wout_ref, ropeq_ref, ropek_ref, decay_ref, xi_ref, zeta_ref,
                  gn_g_ref, gn_b_ref, zln_g_ref, zln_b_ref, ws_ref, bs_ref, ln_g_ref, ln_b_ref,
                  o_ref, state_ref, mix_ref, *, gamma_block, n_seq):
    i = pl.program_id(0)

    def start_current(first_step, finish):
        _mix_tile(x_ref, win_ref, ropeq_ref, ropek_ref, decay_ref, xi_ref, zeta_ref,
                  gn_g_ref, gn_b_ref, zln_g_ref, zln_b_ref, ws_ref, bs_ref, state_ref,
                  mix_ref, finish,
                  first_step=first_step, first_of_sequence=i % n_seq == 0, gamma_block=gamma_block)

    def finish_previous():
        norms = []
        for lo in range(0, MIX_TOKENS, MIX_OUT_ROWS):
            y = _dot(mix_ref[lo:lo + MIX_OUT_ROWS, :], wout_ref[...])
            norms += _row_group_norms(lambda rows: DEEPNORM_ALPHA * xprev_ref[rows, :], y, lo,
                                      ln_g_ref, ln_b_ref, o_ref)
        return norms

    _skewed_steps(start_current, finish_previous)


def _mix_tile(x_ref, win_ref, ropeq_ref, ropek_ref, decay_ref, xi_ref, zeta_ref,
              gn_g_ref, gn_b_ref, zln_g_ref, zln_b_ref, ws_ref, bs_ref, state_ref, mix_ref, finish,
              *, first_step, first_of_sequence, gamma_block):
    xb = x_ref[...].astype(BF16)
    blocks = range(MIX_TOKENS // RET_BLOCK)
    block_rows = [slice(c * RET_BLOCK, (c + 1) * RET_BLOCK) for c in blocks]
    half = RET_WIDTH // 2
    contract_last = (((1,), (1,)), ((), ()))
    contract_first = (((0,), (0,)), ((), ()))

    def proj(k):
        return _dot(xb, win_ref[:, k * RET_WIDTH:(k + 1) * RET_WIDTH])

    def proj_half(k, part):
        lo = k * RET_WIDTH + part * half
        return _dot(xb, win_ref[:, lo:lo + half])

    def unit_cols(halves, i):
        return halves[i // 2][:, (i % 2) * HEAD_DIM:(i % 2 + 1) * HEAD_DIM]

    def head_scores(h):
        cols = slice(h * HEAD_DIM, (h + 1) * HEAD_DIM)
        qr, scores, kv = [], [], []
        for rows in block_rows:
            qc = _rope(q[rows, cols], ropeq_ref[rows, :HEAD_DIM], ropeq_ref[rows, HEAD_DIM:]).astype(BF16)
            kc = _rope(k[rows, cols], ropek_ref[rows, :HEAD_DIM], ropek_ref[rows, HEAD_DIM:]).astype(BF16)
            qr.append(qc)
            scores.append(lax.dot_general(qc, kc, contract_last, preferred_element_type=F32))
            kv.append(lax.dot_general(kc, (v[rows, cols] * zeta_ref[h]).astype(BF16), contract_first,
                                      preferred_element_type=F32))
        return qr, scores, kv

    def head_retention(h, qr, scores, kv):
        cols = slice(h * HEAD_DIM, (h + 1) * HEAD_DIM)
        if first_step:
            state = jnp.zeros((HEAD_DIM, HEAD_DIM), F32)
        else:
            state = state_ref[h]
            state = jnp.where(first_of_sequence, jnp.zeros_like(state), state)
        ret = []
        for c in blocks:
            decayed = (scores[c] * decay_ref[h]).astype(BF16)
            intra = _dot(decayed, v[block_rows[c], cols].astype(BF16))
            cross = _dot(qr[c], state.astype(BF16)) * xi_ref[h]
            ret.append(intra + cross)
            state = gamma_block[h] * state + kv[c]
        state_ref[h] = state
        return ret

    def head_gate(h, ret, g_halves):
        cols = slice(h * HEAD_DIM, (h + 1) * HEAD_DIM)
        gate_cols = unit_cols(g_halves, h)
        for c in blocks:
            gate = gate_cols[block_rows[c], :]
            normed = _layer_norm(ret[c], gn_g_ref[:, cols], gn_b_ref[:, cols])
            mix_ref[block_rows[c], cols] = (gate * jax.nn.sigmoid(gate) * normed).astype(BF16)

    n_sub = MIX_TOKENS // GMLP_CHUNK
    row_id = lax.broadcasted_iota(jnp.int32, (GMLP_CHUNK, GMLP_CHUNK), 0)
    col_id = lax.broadcasted_iota(jnp.int32, (GMLP_CHUNK, GMLP_CHUNK), 1)
    causal = row_id >= col_id

    def group_mix(gi, z_halves):
        cols = slice(gi * HEAD_DIM, (gi + 1) * HEAD_DIM)
        zn = _layer_norm(_gelu(unit_cols(z_halves, gi)), zln_g_ref[:, cols], zln_b_ref[:, cols]).astype(BF16)
        zcat = jnp.concatenate([zn[s * GMLP_CHUNK:(s + 1) * GMLP_CHUNK] for s in range(n_sub)], axis=1)
        w = jnp.where(causal, ws_ref[gi], 0.0).astype(BF16)
        return _dot(w, zcat)

    def group_gate(gi, mixed, u_halves):
        u_cols = unit_cols(u_halves, gi)
        for s in range(n_sub):
            rows = slice(s * GMLP_CHUNK, (s + 1) * GMLP_CHUNK)
            m = mixed[:, s * GMLP_CHUNK:(s + 1) * GMLP_CHUNK] + bs_ref[gi]
            mix_ref[rows, RET_WIDTH + gi * HEAD_DIM:RET_WIDTH + (gi + 1) * HEAD_DIM] = (
                _gelu(u_cols[rows, :]) * m).astype(BF16)

    q, k, v = proj(0), proj(1), proj(2)
    for norm_rows in finish():
        norm_rows()

    g_halves = [proj_half(3, 0), proj_half(3, 1)]
    fillers = [(5, 0), (5, 1), (4, 0), (4, 1)]
    slabs = []
    for h in range(RET_HEADS):
        s = head_scores(h)
        slabs.append(proj_half(*fillers[h]))
        r = head_retention(h, *s)
        head_gate(h, r, g_halves)
    z_halves, u_halves = slabs[:2], slabs[2:]
    mixed = [group_mix(gi, z_halves) for gi in range(GMLP_GROUPS)]
    for gi in range(GMLP_GROUPS):
        group_gate(gi, mixed[gi], u_halves)


def _retention_tables(block):
    log_gamma = np.log1p(-(2.0 ** (-5.0 - np.arange(RET_HEADS, dtype=np.float64))))
    idx = np.arange(block, dtype=np.float64)
    rel = idx[:, None] - idx[None, :]
    decay = np.where(rel[None] >= 0, np.exp(np.maximum(rel, 0.0)[None] * log_gamma[:, None, None]), 0.0)
    xi = np.exp((idx + 1.0)[None, :] * log_gamma[:, None])
    zeta = np.exp((block - 1.0 - idx)[None, :] * log_gamma[:, None])
    widen = lambda t: np.broadcast_to(t[:, :, None], (RET_HEADS, block, HEAD_DIM))
    return tuple(jnp.asarray(t, F32) for t in (decay, widen(xi), widen(zeta)))


def _rope_tables(seq):
    inv_freq = ROPE_BASE ** (-np.arange(0, HEAD_DIM, 2, dtype=np.float64) / HEAD_DIM)
    ang = np.arange(seq, dtype=np.float64)[:, None] * inv_freq[None, :]
    cos, sin = np.cos(ang), np.sin(ang)
    table = np.concatenate([cos, cos, -sin, sin], axis=-1)
    return jnp.asarray(table, F32), jnp.asarray(table * (HEAD_DIM ** -0.5), F32)


def _mixer_call(x, batch, seq, w_in, w_out, gn_g, gn_b, zln_g, zln_b, w_s, b_s, ln_g, ln_b):
    n_seq = seq // MIX_TOKENS
    n_tiles = batch * n_seq
    rope_q, rope_k = _rope_tables(seq)
    decay, xi, zeta = _retention_tables(RET_BLOCK)
    gamma_block = tuple(float((1.0 - 2.0 ** (-5.0 - h)) ** RET_BLOCK) for h in range(RET_HEADS))
    bs_full = jnp.broadcast_to(b_s[:, :, None], (GMLP_GROUPS, GMLP_CHUNK, HEAD_DIM))

    def resident(shape):
        zeros = (0,) * len(shape)
        return pl.BlockSpec(shape, lambda i: zeros, pipeline_mode=pl.Buffered(1))

    tok_in = pl.BlockSpec((MIX_TOKENS, D_MODEL), lambda i: (jnp.minimum(i, n_tiles - 1), 0))
    previous = lambda i: (jnp.maximum(i - 1, 0), 0)
    tok_prev = pl.BlockSpec((MIX_TOKENS, D_MODEL), previous)
    tok_out = pl.BlockSpec((MIX_TOKENS, D_MODEL), previous)
    pos = pl.BlockSpec((MIX_TOKENS, 2 * HEAD_DIM), lambda i: (jnp.minimum(i, n_tiles - 1) % n_seq, 0))
    return pl.pallas_call(
        functools.partial(_mixer_kernel, gamma_block=gamma_block, n_seq=n_seq),
        out_shape=jax.ShapeDtypeStruct(x.shape, F32),
        grid=(n_tiles + 1,),
        in_specs=[
            tok_in, tok_prev,
            resident((D_MODEL, IN_WIDTH)),
            resident((D_MODEL, D_MODEL)),
            pos, pos,
            resident((RET_HEADS, RET_BLOCK, RET_BLOCK)),
            resident((RET_HEADS, RET_BLOCK, HEAD_DIM)),
            resident((RET_HEADS, RET_BLOCK, HEAD_DIM)),
            resident((1, RET_WIDTH)), resident((1, RET_WIDTH)),
            resident((1, GMLP_WIDTH)), resident((1, GMLP_WIDTH)),
            resident((GMLP_GROUPS, GMLP_CHUNK, GMLP_CHUNK)),
            resident((GMLP_GROUPS, GMLP_CHUNK, HEAD_DIM)),
            resident((1, D_MODEL)), resident((1, D_MODEL)),
        ],
        out_specs=tok_out,
        scratch_shapes=[
            pltpu.VMEM((RET_HEADS, HEAD_DIM, HEAD_DIM), F32),
            pltpu.VMEM((MIX_TOKENS, D_MODEL), BF16),
        ],
        compiler_params=pltpu.CompilerParams(
            dimension_semantics=("arbitrary",), vmem_limit_bytes=VMEM_LIMIT_BYTES),
        name="mixer_ln",
    )(x, x, w_in, w_out, rope_q, rope_k, decay, xi, zeta, gn_g, gn_b, zln_g, zln_b, w_s, bs_full, ln_g, ln_b)


def kernel(x, ffn1_w_in, ffn1_w_out, ln1_g, ln1_b, mix_w_in, ret_gn_g, ret_gn_b, gmlp_ln_g, gmlp_ln_b,
           gmlp_w_s, gmlp_b_s, mix_w_out, ln2_g, ln2_b, ffn2_w_in, ffn2_w_out, ln3_g, ln3_b):
    batch, seq, d = x.shape
    assert d == D_MODEL and seq % MIX_TOKENS == 0 and (batch * seq) % FFN_TOKENS == 0
    assert MIX_TOKENS % RET_BLOCK == 0 and RET_BLOCK % CHUNK == 0 and MIX_TOKENS % GMLP_CHUNK == 0
    row = lambda t: t.reshape(1, -1)
    h = x.reshape(batch * seq, d)
    for l in range(DEPTH):
        h, mix_in, mix_out, ffn2_in, ffn2_out = _ffn_call(
            h, ffn1_w_in[l].astype(BF16), ffn1_w_out[l].astype(BF16), row(ln1_g[l]), row(ln1_b[l]),
            cast_weights=(mix_w_in[l], mix_w_out[l], ffn2_w_in[l], ffn2_w_out[l]))
        h = _mixer_call(h, batch, seq, mix_in, mix_out,
                        row(ret_gn_g[l]), row(ret_gn_b[l]), row(gmlp_ln_g[l]), row(gmlp_ln_b[l]),
                        gmlp_w_s[l], gmlp_b_s[l], row(ln2_g[l]), row(ln2_b[l]))
        h, = _ffn_call(h, ffn2_in, ffn2_out, row(ln3_g[l]), row(ln3_b[l]))
    return h.reshape(batch, seq, d)
```

```python
import functools

import jax
import jax.numpy as jnp
import numpy as np
from jax import lax
from jax.experimental import pallas as pl
from jax.experimental.pallas import tpu as pltpu

F32 = jnp.float32
BF16 = jnp.bfloat16

D_MODEL = 1024
DEPTH = 1
CHUNK = 64
RET_HEADS = 4
HEAD_DIM = 128
RET_WIDTH = RET_HEADS * HEAD_DIM
GMLP_GROUPS = 4
GMLP_WIDTH = GMLP_GROUPS * HEAD_DIM
GMLP_CHUNK = 128
IN_WIDTH = 4 * RET_WIDTH + 2 * GMLP_WIDTH
D_FF = 2816
ROPE_BASE = 10000.0
LN_EPS = 1e-5
DEEPNORM_ALPHA = (2.0 * DEPTH) ** 0.25

SUBLANES = 8
BF16_ROWS = 16
MXU_DIM = 256
VMEM_LIMIT_BYTES = 56 * 1024 * 1024

FFN_TOKENS = 1024
FFN_CHUNK = MXU_DIM
MIX_TOKENS = 512
MIX_OUT_ROWS = 256
RET_BLOCK = 256
NORM_ROWS = SUBLANES
FFN_OUT_SPLITS = (208, 416, 624, 832)


def _layer_norm(r, g, b):
    mu = jnp.mean(r, axis=-1, keepdims=True)
    d = r - mu
    var = jnp.mean(d * d, axis=-1, keepdims=True)
    return d * lax.rsqrt(var + LN_EPS) * g + b


def _dot(a, b):
    return jnp.dot(a, b, preferred_element_type=F32)


def _skewed_steps(start_current, finish_previous):
    i = pl.program_id(0)
    last = pl.num_programs(0) - 1

    @pl.when(i == 0)
    def _():
        start_current(True, lambda: [])

    @pl.when(jnp.logical_and(i > 0, i < last))
    def _():
        start_current(False, finish_previous)

    @pl.when(i == last)
    def _():
        for norm_rows in finish_previous():
            norm_rows()


def _row_group_norms(resid, y, first_row, g_ref, b_ref, o_ref):
    def norm_rows(lo):
        rows = slice(first_row + lo, first_row + lo + NORM_ROWS)
        o_ref[rows, :] = _layer_norm(resid(rows) + y[lo:lo + NORM_ROWS, :], g_ref[...], b_ref[...])
    return [functools.partial(norm_rows, lo) for lo in range(0, y.shape[0], NORM_ROWS)]


def _ffn_kernel(n_cast, bf16_copy, x_ref, win_ref, wout_ref, g_ref, b_ref, *refs):
    cast_src, o_ref, act_ref = refs[:n_cast], refs[n_cast], refs[-1]
    ob_ref = refs[n_cast + 1] if bf16_copy else None
    cast_dst = refs[n_cast + 1 + bf16_copy:-1]
    xb = x_ref[...].astype(BF16)
    for j in range(D_FF // FFN_CHUNK):
        lo = j * FFN_CHUNK
        gate = _dot(xb, win_ref[:, lo:lo + FFN_CHUNK])
        up = _dot(xb, win_ref[:, D_FF + lo:D_FF + lo + FFN_CHUNK])
        act_ref[:, lo:lo + FFN_CHUNK] = (gate * jax.nn.sigmoid(gate) * up).astype(BF16)
    for src_ref, dst_ref in zip(cast_src, cast_dst):
        dst_ref[...] = src_ref[...].astype(BF16)
    for lo, hi in zip((0,) + FFN_OUT_SPLITS, FFN_OUT_SPLITS + (FFN_TOKENS,)):
        rows = slice(lo, hi)
        y = _dot(act_ref[rows, :], wout_ref[...])
        out = _layer_norm(DEEPNORM_ALPHA * x_ref[rows, :] + 0.5 * y, g_ref[...], b_ref[...])
        o_ref[rows, :] = out
        if bf16_copy:
            ob_ref[rows, :] = out.astype(BF16)


def _row_chunk_spec(shape, n_steps):
    rows = next(r for r in range(BF16_ROWS, shape[0] + 1, BF16_ROWS)
                if shape[0] % r == 0 and shape[0] // r <= n_steps)
    last = shape[0] // rows - 1
    return pl.BlockSpec((rows, shape[1]), lambda i: (jnp.minimum(i, last), 0))


def _ffn_call(x, w_in, w_out, ln_g, ln_b, cast_weights=(), bf16_copy=False):
    n = x.shape[0]
    n_steps = n // FFN_TOKENS
    const = lambda i: (0, 0)
    resident = functools.partial(pl.BlockSpec, index_map=const, pipeline_mode=pl.Buffered(1))
    tile = lambda: pl.BlockSpec((FFN_TOKENS, D_MODEL), lambda i: (i, 0))
    chunks = lambda: [_row_chunk_spec(w.shape, n_steps) for w in cast_weights]
    return pl.pallas_call(
        functools.partial(_ffn_kernel, len(cast_weights), int(bf16_copy)),
        out_shape=[jax.ShapeDtypeStruct((n, D_MODEL), F32)] +
                  [jax.ShapeDtypeStruct((n, D_MODEL), BF16)] * bf16_copy +
                  [jax.ShapeDtypeStruct(w.shape, BF16) for w in cast_weights],
        grid=(n_steps,),
        in_specs=[
            tile(),
            resident((D_MODEL, 2 * D_FF)),
            resident((D_FF, D_MODEL)),
            resident((1, D_MODEL)),
            resident((1, D_MODEL)),
        ] + chunks(),
        out_specs=[tile()] + [tile()] * bf16_copy + chunks(),
        scratch_shapes=[pltpu.VMEM((FFN_TOKENS, D_FF), BF16)],
        compiler_params=pltpu.CompilerParams(
            dimension_semantics=("arbitrary",), vmem_limit_bytes=VMEM_LIMIT_BYTES),
        name="ffn_ln",
    )(x, w_in, w_out, ln_g, ln_b, *cast_weights)


def _rope(t, cos, sin_signed):
    return t * cos + pltpu.roll(t, HEAD_DIM // 2, 1) * sin_signed


def _gelu(t):
    return 0.5 * t * (1.0 + lax.erf(t * (0.5 ** 0.5)))


def _mixer_kernel(x_ref, xprev_ref, win_ref, wout_ref, ropeq_ref, ropek_ref, decay_ref, xi_ref, zeta_ref,
                  gn_g_ref, gn_b_ref, zln_g_ref, zln_b_ref, ws_ref, bs_ref, ln_g_ref, ln_b_ref,
                  o_ref, state_ref, mix_ref, *, gamma_block, n_seq):
    i = pl.program_id(0)

    def start_current(first_step, finish):
        _mix_tile(x_ref, win_ref, ropeq_ref, ropek_ref, decay_ref, xi_ref, zeta_ref,
                  gn_g_ref, gn_b_ref, zln_g_ref, zln_b_ref, ws_ref, bs_ref, state_ref,
                  mix_ref, finish,
                  first_step=first_step, first_of_sequence=i % n_seq == 0, gamma_block=gamma_block)

    def finish_previous():
        norms = []
        for lo in range(0, MIX_TOKENS, MIX_OUT_ROWS):
            y = _dot(mix_ref[lo:lo + MIX_OUT_ROWS, :], wout_ref[...])
            norms += _row_group_norms(lambda rows: DEEPNORM_ALPHA * xprev_ref[rows, :], y, lo,
                                      ln_g_ref, ln_b_ref, o_ref)
        return norms

    _skewed_steps(start_current, finish_previous)


def _mix_tile(x_ref, win_ref, ropeq_ref, ropek_ref, decay_ref, xi_ref, zeta_ref,
              gn_g_ref, gn_b_ref, zln_g_ref, zln_b_ref, ws_ref, bs_ref, state_ref, mix_ref, finish,
              *, first_step, first_of_sequence, gamma_block):
    xb = x_ref[...].astype(BF16)
    blocks = range(MIX_TOKENS // RET_BLOCK)
    block_rows = [slice(c * RET_BLOCK, (c + 1) * RET_BLOCK) for c in blocks]
    half = RET_WIDTH // 2
    contract_last = (((1,), (1,)), ((), ()))
    contract_first = (((0,), (0,)), ((), ()))

    def proj(k):
        return _dot(xb, win_ref[:, k * RET_WIDTH:(k + 1) * RET_WIDTH])

    def proj_half(k, part):
        lo = k * RET_WIDTH + part * half
        return _dot(xb, win_ref[:, lo:lo + half])

    def unit_cols(halves, i):
        return halves[i // 2][:, (i % 2) * HEAD_DIM:(i % 2 + 1) * HEAD_DIM]

    def head_scores(h):
        cols = slice(h * HEAD_DIM, (h + 1) * HEAD_DIM)
        qr, scores, kv = [], [], []
        for rows in block_rows:
            qc = _rope(q[rows, cols], ropeq_ref[rows, :HEAD_DIM], ropeq_ref[rows, HEAD_DIM:]).astype(BF16)
            kc = _rope(k[rows, cols], ropek_ref[rows, :HEAD_DIM], ropek_ref[rows, HEAD_DIM:]).astype(BF16)
            qr.append(qc)
            scores.append(lax.dot_general(qc, kc, contract_last, preferred_element_type=F32))
            kv.append(lax.dot_general(kc, (v[rows, cols] * zeta_ref[h]).astype(BF16), contract_first,
                                      preferred_element_type=F32))
        return qr, scores, kv

    def head_retention(h, qr, scores, kv):
        cols = slice(h * HEAD_DIM, (h + 1) * HEAD_DIM)
        if first_step:
            state = jnp.zeros((HEAD_DIM, HEAD_DIM), F32)
        else:
            state = state_ref[h]
            state = jnp.where(first_of_sequence, jnp.zeros_like(state), state)
        ret = []
        for c in blocks:
            decayed = (scores[c] * decay_ref[h]).astype(BF16)
            intra = _dot(decayed, v[block_rows[c], cols].astype(BF16))
            cross = _dot(qr[c], state.astype(BF16)) * xi_ref[h]
            ret.append(intra + cross)
            state = gamma_block[h] * state + kv[c]
        state_ref[h] = state
        return ret

    def head_gate(h, ret, g_halves):
        cols = slice(h * HEAD_DIM, (h + 1) * HEAD_DIM)
        gate_cols = unit_cols(g_halves, h)
        for c in blocks:
            gate = gate_cols[block_rows[c], :]
            normed = _layer_norm(ret[c], gn_g_ref[:, cols], gn_b_ref[:, cols])
            mix_ref[block_rows[c], cols] = (gate * jax.nn.sigmoid(gate) * normed).astype(BF16)

    n_sub = MIX_TOKENS // GMLP_CHUNK
    row_id = lax.broadcasted_iota(jnp.int32, (GMLP_CHUNK, GMLP_CHUNK), 0)
    col_id = lax.broadcasted_iota(jnp.int32, (GMLP_CHUNK, GMLP_CHUNK), 1)
    causal = row_id >= col_id

    def group_mix(gi, z_halves):
        cols = slice(gi * HEAD_DIM, (gi + 1) * HEAD_DIM)
        zn = _layer_norm(_gelu(unit_cols(z_halves, gi)), zln_g_ref[:, cols], zln_b_ref[:, cols]).astype(BF16)
        zcat = jnp.concatenate([zn[s * GMLP_CHUNK:(s + 1) * GMLP_CHUNK] for s in range(n_sub)], axis=1)
        w = jnp.where(causal, ws_ref[gi], 0.0).astype(BF16)
        return _dot(w, zcat)

    def group_gate(gi, mixed, u_halves):
        u_cols = unit_cols(u_halves, gi)
        for s in range(n_sub):
            rows = slice(s * GMLP_CHUNK, (s + 1) * GMLP_CHUNK)
            m = mixed[:, s * GMLP_CHUNK:(s + 1) * GMLP_CHUNK] + bs_ref[gi]
            mix_ref[rows, RET_WIDTH + gi * HEAD_DIM:RET_WIDTH + (gi + 1) * HEAD_DIM] = (
                _gelu(u_cols[rows, :]) * m).astype(BF16)

    q, k, v = proj(0), proj(1), proj(2)
    for norm_rows in finish():
        norm_rows()

    g_halves = [proj_half(3, 0), proj_half(3, 1)]
    fillers = [(5, 0), (5, 1), (4, 0), (4, 1)]
    slabs = []
    for h in range(RET_HEADS):
        s = head_scores(h)
        slabs.append(proj_half(*fillers[h]))
        r = head_retention(h, *s)
        head_gate(h, r, g_halves)
    z_halves, u_halves = slabs[:2], slabs[2:]
    mixed = [group_mix(gi, z_halves) for gi in range(GMLP_GROUPS)]
    for gi in range(GMLP_GROUPS):
        group_gate(gi, mixed[gi], u_halves)


def _retention_tables(block):
    log_gamma = np.log1p(-(2.0 ** (-5.0 - np.arange(RET_HEADS, dtype=np.float64))))
    idx = np.arange(block, dtype=np.float64)
    rel = idx[:, None] - idx[None, :]
    decay = np.where(rel[None] >= 0, np.exp(np.maximum(rel, 0.0)[None] * log_gamma[:, None, None]), 0.0)
    xi = np.exp((idx + 1.0)[None, :] * log_gamma[:, None])
    zeta = np.exp((block - 1.0 - idx)[None, :] * log_gamma[:, None])
    widen = lambda t: np.broadcast_to(t[:, :, None], (RET_HEADS, block, HEAD_DIM))
    return tuple(jnp.asarray(t, F32) for t in (decay, widen(xi), widen(zeta)))


def _rope_tables(seq):
    inv_freq = ROPE_BASE ** (-np.arange(0, HEAD_DIM, 2, dtype=np.float64) / HEAD_DIM)
    ang = np.arange(seq, dtype=np.float64)[:, None] * inv_freq[None, :]
    cos, sin = np.cos(ang), np.sin(ang)
    table = np.concatenate([cos, cos, -sin, sin], axis=-1)
    return jnp.asarray(table, F32), jnp.asarray(table * (HEAD_DIM ** -0.5), F32)


def _mixer_call(x, x_bf16, batch, seq, w_in, w_out, gn_g, gn_b, zln_g, zln_b, w_s, b_s, ln_g, ln_b):
    n_seq = seq // MIX_TOKENS
    n_tiles = batch * n_seq
    rope_q, rope_k = _rope_tables(seq)
    decay, xi, zeta = _retention_tables(RET_BLOCK)
    gamma_block = tuple(float((1.0 - 2.0 ** (-5.0 - h)) ** RET_BLOCK) for h in range(RET_HEADS))
    bs_full = jnp.broadcast_to(b_s[:, :, None], (GMLP_GROUPS, GMLP_CHUNK, HEAD_DIM))

    def resident(shape):
        zeros = (0,) * len(shape)
        return pl.BlockSpec(shape, lambda i: zeros, pipeline_mode=pl.Buffered(1))

    tok_in = pl.BlockSpec((MIX_TOKENS, D_MODEL), lambda i: (jnp.minimum(i, n_tiles - 1), 0))
    previous = lambda i: (jnp.maximum(i - 1, 0), 0)
    tok_prev = pl.BlockSpec((MIX_TOKENS, D_MODEL), previous)
    tok_out = pl.BlockSpec((MIX_TOKENS, D_MODEL), previous)
    pos = pl.BlockSpec((MIX_TOKENS, 2 * HEAD_DIM), lambda i: (jnp.minimum(i, n_tiles - 1) % n_seq, 0))
    return pl.pallas_call(
        functools.partial(_mixer_kernel, gamma_block=gamma_block, n_seq=n_seq),
        out_shape=jax.ShapeDtypeStruct(x.shape, F32),
        grid=(n_tiles + 1,),
        in_specs=[
            tok_in, tok_prev,
            resident((D_MODEL, IN_WIDTH)),
            resident((D_MODEL, D_MODEL)),
            pos, pos,
            resident((RET_HEADS, RET_BLOCK, RET_BLOCK)),
            resident((RET_HEADS, RET_BLOCK, HEAD_DIM)),
            resident((RET_HEADS, RET_BLOCK, HEAD_DIM)),
            resident((1, RET_WIDTH)), resident((1, RET_WIDTH)),
            resident((1, GMLP_WIDTH)), resident((1, GMLP_WIDTH)),
            resident((GMLP_GROUPS, GMLP_CHUNK, GMLP_CHUNK)),
            resident((GMLP_GROUPS, GMLP_CHUNK, HEAD_DIM)),
            resident((1, D_MODEL)), resident((1, D_MODEL)),
        ],
        out_specs=tok_out,
        scratch_shapes=[
            pltpu.VMEM((RET_HEADS, HEAD_DIM, HEAD_DIM), F32),
            pltpu.VMEM((MIX_TOKENS, D_MODEL), BF16),
        ],
        compiler_params=pltpu.CompilerParams(
            dimension_semantics=("arbitrary",), vmem_limit_bytes=VMEM_LIMIT_BYTES),
        name="mixer_ln",
    )(x_bf16, x, w_in, w_out, rope_q, rope_k, decay, xi, zeta, gn_g, gn_b, zln_g, zln_b, w_s, bs_full, ln_g, ln_b)


def kernel(x, ffn1_w_in, ffn1_w_out, ln1_g, ln1_b, mix_w_in, ret_gn_g, ret_gn_b, gmlp_ln_g, gmlp_ln_b,
           gmlp_w_s, gmlp_b_s, mix_w_out, ln2_g, ln2_b, ffn2_w_in, ffn2_w_out, ln3_g, ln3_b):
    batch, seq, d = x.shape
    assert d == D_MODEL and seq % MIX_TOKENS == 0 and (batch * seq) % FFN_TOKENS == 0
    assert MIX_TOKENS % RET_BLOCK == 0 and RET_BLOCK % CHUNK == 0 and MIX_TOKENS % GMLP_CHUNK == 0
    row = lambda t: t.reshape(1, -1)
    h = x.reshape(batch * seq, d)
    for l in range(DEPTH):
        h, h_bf16, mix_in, mix_out, ffn2_in, ffn2_out = _ffn_call(
            h, ffn1_w_in[l].astype(BF16), ffn1_w_out[l].astype(BF16), row(ln1_g[l]), row(ln1_b[l]),
            cast_weights=(mix_w_in[l], mix_w_out[l], ffn2_w_in[l], ffn2_w_out[l]), bf16_copy=True)
        h = _mixer_call(h, h_bf16, batch, seq, mix_in, mix_out,
                        row(ret_gn_g[l]), row(ret_gn_b[l]), row(gmlp_ln_g[l]), row(gmlp_ln_b[l]),
                        gmlp_w_s[l], gmlp_b_s[l], row(ln2_g[l]), row(ln2_b[l]))
        h, = _ffn_call(h, ffn2_in, ffn2_out, row(ln3_g[l]), row(ln3_b[l]))
    return h.reshape(batch, seq, d)
```
